```python
import jax, jax.numpy as jnp
from jax import lax
import numpy as np

D_MODEL = 1024
BATCH = 8
SEQ = 4096
DEPTH = 4

N_META = 16
N_MIXERS = 3
D_FF = 2816
NORMS_PER_LAYER = 6
RMS_EPS = 1e-6
CONV_WIDTH = 3
GLA_HEADS = 4
GLA_DK = D_MODEL // 2
GLA_DV = D_MODEL
GLA_HK = GLA_DK // GLA_HEADS
GLA_HV = GLA_DV // GLA_HEADS
GLA_RANK = 16
GLA_GATE_NORMALIZER = 16.0
GLA_CHUNK = 64
GLA_IN = 2 * GLA_DK + 2 * GLA_DV + GLA_RANK
SB_HEADS = 16
SB_HEAD_DIM = D_MODEL // SB_HEADS
SB_BLOCK = 128
N_A = (DEPTH + 2) // 3
N_B = (DEPTH + 1) // 3
N_C = DEPTH // 3

kernel_name = 'hybrid_conv_gla_stickbreaking_macaron'


def rmsnorm(x, g):
    xf = x.astype(jnp.float32)
    xf = xf * lax.rsqrt(jnp.mean(xf * xf, axis=-1, keepdims=True) + RMS_EPS)
    return (xf * g.astype(jnp.float32)).astype(x.dtype)


def swiglu(x, w_in, w_out):
    gate, up = jnp.split(x @ w_in, 2, axis=-1)
    return (jax.nn.silu(gate) * up) @ w_out


def short_conv_mixer(x, w_in, conv_w, w_out):
    L = x.shape[1]
    b, c, h = jnp.split(x @ w_in, 3, axis=-1)
    u = c * h
    up = jnp.pad(u, ((0, 0), (CONV_WIDTH - 1, 0), (0, 0)))
    conv = sum(conv_w[tap] * up[:, tap:tap + L] for tap in range(CONV_WIDTH))
    return (b * conv) @ w_out


def gla_mixer(x, w_in, w_gate_up, b_gate, norm_g, w_out):
    Bsz, L, _ = x.shape
    f32 = jnp.float32
    q, k, v, g, a = jnp.split(x @ w_in, [GLA_DK, 2 * GLA_DK, 2 * GLA_DK + GLA_DV, 2 * GLA_DK + 2 * GLA_DV], axis=-1)
    log_a = jax.nn.log_sigmoid((a @ w_gate_up + b_gate).astype(f32)) / GLA_GATE_NORMALIZER
    pad = GLA_CHUNK - N_META
    padf = lambda t: jnp.pad(t.astype(f32), ((0, 0), (pad, 0), (0, 0)))
    q, k, v, log_a = padf(q) * (GLA_HK ** -0.5), padf(k), padf(v), padf(log_a)
    n_chunks = (L + pad) // GLA_CHUNK

    def chunked(t, hd):
        return t.reshape(Bsz, n_chunks, GLA_CHUNK, GLA_HEADS, hd).transpose(1, 0, 3, 2, 4)

    qc, kc, vc = chunked(q, GLA_HK), chunked(k, GLA_HK), chunked(v, GLA_HV)
    bc = jnp.cumsum(chunked(log_a, GLA_HK), axis=3)
    causal = jnp.tril(jnp.ones((GLA_CHUNK, GLA_CHUNK), bool))[:, :, None]

    def step(S, inp):
        qi, ki, vi, bi = inp
        diff = bi[:, :, :, None, :] - bi[:, :, None, :, :]
        decay = jnp.exp(jnp.where(causal, diff, -jnp.inf))
        att = jnp.einsum('bhtd,bhsd,bhtsd->bhts', qi, ki, decay)
        o = jnp.einsum('bhts,bhsv->bhtv', att, vi) + jnp.einsum('bhtd,bhdv->bhtv', qi * jnp.exp(bi), S)
        b_last = bi[:, :, -1:, :]
        S = jnp.exp(b_last[:, :, 0, :, None]) * S + jnp.einsum('bhsd,bhsv->bhdv', ki * jnp.exp(b_last - bi), vi)
        return S, o

    S0 = jnp.zeros((Bsz, GLA_HEADS, GLA_HK, GLA_HV), f32)
    _, o = lax.scan(step, S0, (qc, kc, vc, bc))
    o = o.transpose(1, 0, 3, 2, 4).reshape(Bsz, n_chunks * GLA_CHUNK, GLA_HEADS, GLA_HV)[:, pad:]
    o = rmsnorm(o, norm_g).reshape(Bsz, L, GLA_DV).astype(x.dtype)
    return (o * jax.nn.silu(g)) @ w_out


def stick_breaking_mixer(x, w_in, w_out):
    Bsz, L, _ = x.shape
    f32 = jnp.float32
    q, k, v = jnp.split(x @ w_in, 3, axis=-1)
    heads = lambda t: t.reshape(Bsz, L, SB_HEADS, SB_HEAD_DIM).transpose(0, 2, 1, 3)
    q = heads(q).astype(f32) * (SB_HEAD_DIM ** -0.5)
    k, v = heads(k).astype(f32), heads(v)
    bounds = [(0, N_META)] + [(N_META + i * SB_BLOCK, N_META + (i + 1) * SB_BLOCK)
                              for i in range((L - N_META) // SB_BLOCK)]
    outs = []
    for t0, t1 in bounds:
        z = jnp.einsum('bhtd,bhsd->bhts', q[:, :, t0:t1], k[:, :, :t1])
        strict = jnp.arange(t1)[None, :] < jnp.arange(t0, t1)[:, None]
        log_1m = jnp.where(strict, jax.nn.log_sigmoid(-z), 0.0)
        tail = lax.cumsum(log_1m, axis=3, reverse=True) - log_1m
        w = jnp.where(strict, jnp.exp(jax.nn.log_sigmoid(z) + tail), 0.0)
        outs.append(jnp.einsum('bhts,bhsd->bhtd', w.astype(v.dtype), v[:, :, :t1]))
    o = jnp.concatenate(outs, axis=2).transpose(0, 2, 1, 3).reshape(Bsz, L, D_MODEL)
    return o @ w_out


def setup_inputs(seed: int = 0) -> dict:
    key = jax.random.key(seed)
    ks = jax.random.split(key, 16)
    f32 = jnp.float32
    nrm = lambda kk, shape, fan_in: jax.random.normal(kk, shape, f32) * (fan_in ** -0.5)
    return {
        'x': jax.random.normal(ks[0], (BATCH, SEQ, D_MODEL), f32),
        'meta_tokens': jax.random.normal(ks[1], (N_META, D_MODEL), f32),
        'norm_gains': 1.0 + 0.05 * jax.random.normal(ks[2], (DEPTH, NORMS_PER_LAYER, D_MODEL), f32),
        'ffn_w_in': nrm(ks[3], (DEPTH, 2, D_MODEL, 2 * D_FF), D_MODEL),
        'ffn_w_out': nrm(ks[4], (DEPTH, 2, D_FF, D_MODEL), D_FF),
        'conv_w_in': nrm(ks[5], (N_A, D_MODEL, 3 * D_MODEL), D_MODEL),
        'conv_w': nrm(ks[6], (N_A, CONV_WIDTH, D_MODEL), CONV_WIDTH),
        'conv_w_out': nrm(ks[7], (N_A, D_MODEL, D_MODEL), D_MODEL),
        'gla_w_in': nrm(ks[8], (N_B, D_MODEL, GLA_IN), D_MODEL),
        'gla_w_gate_up': nrm(ks[9], (N_B, GLA_RANK, GLA_DK), GLA_RANK),
        'gla_b_gate': 0.1 * jax.random.normal(ks[10], (N_B, GLA_DK), f32),
        'gla_norm': 1.0 + 0.05 * jax.random.normal(ks[11], (N_B, GLA_HV), f32),
        'gla_w_out': nrm(ks[12], (N_B, GLA_DV, D_MODEL), GLA_DV),
        'sb_w_in': nrm(ks[13], (N_C, D_MODEL, 3 * D_MODEL), D_MODEL),
        'sb_w_out': nrm(ks[14], (N_C, D_MODEL, D_MODEL), D_MODEL),
    }


def reference(x, meta_tokens, norm_gains, ffn_w_in, ffn_w_out, conv_w_in, conv_w, conv_w_out,
              gla_w_in, gla_w_gate_up, gla_b_gate, gla_norm, gla_w_out, sb_w_in, sb_w_out):
    Bsz = x.shape[0]
    meta = jnp.broadcast_to(meta_tokens[None].astype(x.dtype), (Bsz, N_META, D_MODEL))
    h = jnp.concatenate([meta, x], axis=1)
    for i in range(DEPTH):
        g = norm_gains[i]
        h = h + 0.5 * rmsnorm(swiglu(rmsnorm(h, g[0]), ffn_w_in[i, 0], ffn_w_out[i, 0]), g[1])
        kind, j = i % N_MIXERS, i // N_MIXERS
        u = rmsnorm(h, g[2])
        if kind == 0:
            m = short_conv_mixer(u, conv_w_in[j], conv_w[j], conv_w_out[j])
        elif kind == 1:
            m = gla_mixer(u, gla_w_in[j], gla_w_gate_up[j], gla_b_gate[j], gla_norm[j], gla_w_out[j])
        else:
            m = stick_breaking_mixer(u, sb_w_in[j], sb_w_out[j])
        h = h + rmsnorm(m, g[3])
        h = h + 0.5 * rmsnorm(swiglu(rmsnorm(h, g[4]), ffn_w_in[i, 1], ffn_w_out[i, 1]), g[5])
    return h[:, N_META:]
```

```python
import functools

import jax
import jax.numpy as jnp
from jax import lax
from jax.experimental import pallas as pl
from jax.experimental.pallas import tpu as pltpu

F32 = jnp.float32
BF16 = jnp.bfloat16

D_MODEL = 1024
D_FF = 2816
N_META = 16
RMS_EPS = 1e-6
CONV_WIDTH = 3
GLA_HEADS = 4
GLA_HK = 128
GLA_HV = 256
GLA_DK = GLA_HEADS * GLA_HK
GLA_DV = GLA_HEADS * GLA_HV
GLA_RANK = 16
GLA_GATE_NORMALIZER = 16.0
GLA_CHUNK = 64
GLA_SUB = 16
SB_HEAD_DIM = 64
SB_BLOCK = 128
LANES = 128
PAD = SB_BLOCK - N_META

VMEM_LIMIT_BYTES = 56 * 1024 * 1024


def _row_tile(total, candidates=(1024, 512, 384, 256, 128)):
    for c in candidates:
        if total % c == 0:
            return c
    raise ValueError(f"no row tile for {total}")


def _rms(x, g):
    return x * lax.rsqrt(jnp.mean(x * x, axis=-1, keepdims=True) + RMS_EPS) * g


def _dot(a, b):
    return jnp.dot(a, b, preferred_element_type=F32)


def _dot_nt(a, b):
    return lax.dot_general(a, b, (((1,), (1,)), ((), ())), preferred_element_type=F32)


def _dot_tn(a, b):
    return lax.dot_general(a, b, (((0,), (0,)), ((), ())), preferred_element_type=F32)


def _sigmoid(x):
    return 1.0 / (1.0 + jnp.exp(-x))


def _log_sigmoid(x):
    return jnp.minimum(x, 0.0) - jnp.log1p(jnp.exp(-jnp.abs(x)))


def _const_spec(shape):
    return pl.BlockSpec(shape, lambda *_: (0,) * len(shape), pipeline_mode=pl.Buffered(1))


def _params(n_axes):
    return pltpu.CompilerParams(dimension_semantics=("arbitrary",) * n_axes,
                                vmem_limit_bytes=VMEM_LIMIT_BYTES)


def _ffn_body(h_ref, g0_ref, g1_ref, win_ref, wout_ref, o_ref, *, ff_chunk):
    h = h_ref[...]
    xb = _rms(h, g0_ref[...]).astype(BF16)
    y = None
    for c in range(D_FF // ff_chunk):
        lo = c * ff_chunk
        gate = _dot(xb, win_ref[:, lo:lo + ff_chunk])
        up = _dot(xb, win_ref[:, D_FF + lo:D_FF + lo + ff_chunk])
        a = (gate * _sigmoid(gate) * up).astype(BF16)
        part = _dot(a, wout_ref[lo:lo + ff_chunk, :])
        y = part if y is None else y + part
    o_ref[...] = h + 0.5 * _rms(y, g1_ref[...])


def _ffn(h, g_pre, g_post, w_in, w_out, *, tm, ff_chunk=D_FF):
    t = h.shape[0]
    row = pl.BlockSpec((tm, D_MODEL), lambda i: (i, 0))
    return pl.pallas_call(
        functools.partial(_ffn_body, ff_chunk=ff_chunk),
        grid=(t // tm,),
        in_specs=[row, _const_spec((1, D_MODEL)), _const_spec((1, D_MODEL)),
                  _const_spec((D_MODEL, 2 * D_FF)), _const_spec((D_FF, D_MODEL))],
        out_specs=row,
        out_shape=jax.ShapeDtypeStruct(h.shape, F32),
        compiler_params=_params(1),
        name="ffn",
    )(h, g_pre, g_post, w_in, w_out)


def _conv_body(h_ref, halo_ref, g2_ref, g3_ref, cw_ref, win_ref, wout_ref, o_ref, *, tm, lp):
    i = pl.program_id(0)
    h = h_ref[...]
    g2 = g2_ref[...]
    xb = _rms(h, g2).astype(BF16)
    p = _dot(xb, win_ref[...])
    row = lax.broadcasted_iota(jnp.int32, (tm, 1), 0)
    pos = (i * tm + row) % lp
    uu = jnp.where(pos >= PAD, p[:, D_MODEL:2 * D_MODEL] * p[:, 2 * D_MODEL:], 0.0)
    xh = _rms(halo_ref[...], g2).astype(BF16)
    ph = _dot(xh, win_ref[:, D_MODEL:])
    hrow = lax.broadcasted_iota(jnp.int32, (8, 1), 0)
    hpos = (i * tm - 8 + hrow + lp) % lp
    uh = jnp.where(hpos >= PAD, ph[:, :D_MODEL] * ph[:, D_MODEL:], 0.0)
    prev1 = jnp.where(row == 0, uh[7:8], pltpu.roll(uu, 1, 0))
    prev2 = jnp.where(row == 0, uh[6:7], jnp.where(row == 1, uh[7:8], pltpu.roll(uu, 2, 0)))
    prev1 = jnp.where(pos >= PAD + 1, prev1, 0.0)
    prev2 = jnp.where(pos >= PAD + 2, prev2, 0.0)
    cw = cw_ref[...]
    conv = cw[2:3] * uu + cw[1:2] * prev1 + cw[0:1] * prev2
    y = _dot((p[:, :D_MODEL] * conv).astype(BF16), wout_ref[...])
    o_ref[...] = h + _rms(y, g3_ref[...])


def _conv_mixer(h, g_pre, g_post, conv_w, w_in, w_out, *, tm, lp):
    t = h.shape[0]
    row = pl.BlockSpec((tm, D_MODEL), lambda i: (i, 0))
    halo = pl.BlockSpec((8, D_MODEL), lambda i: (jnp.maximum(i * (tm // 8) - 1, 0), 0))
    return pl.pallas_call(
        functools.partial(_conv_body, tm=tm, lp=lp),
        grid=(t // tm,),
        in_specs=[row, halo, _const_spec((1, D_MODEL)), _const_spec((1, D_MODEL)),
                  _const_spec((8, D_MODEL)), _const_spec((D_MODEL, 3 * D_MODEL)),
                  _const_spec((D_MODEL, D_MODEL))],
        out_specs=row,
        out_shape=jax.ShapeDtypeStruct(h.shape, F32),
        compiler_params=_params(1),
        name="conv_mixer",
    )(h, h, g_pre, g_post, conv_w, w_in, w_out)


def _split3(x):
    hi = x.astype(BF16)
    r = x - hi.astype(F32)
    mid = r.astype(BF16)
    lo = (r - mid.astype(F32)).astype(BF16)
    return hi, mid, lo


def _gla_chunk_head(q, k, v, b, s_t):
    n_sub = GLA_CHUNK // GLA_SUB
    sub_row = lax.broadcasted_iota(jnp.int32, (GLA_SUB, 1), 0)
    col = lax.broadcasted_iota(jnp.int32, (1, GLA_CHUNK), 1)
    att_rows = []
    for i in range(n_sub):
        r0 = i * GLA_SUB
        qd, kd, bd = q[r0:r0 + GLA_SUB], k[r0:r0 + GLA_SUB], b[r0:r0 + GLA_SUB]
        blk = jnp.zeros((GLA_SUB, GLA_CHUNK), F32)
        if i > 0:
            b_ref = b[r0 - 1:r0]
            q_t = (qd * jnp.exp(bd - b_ref)).astype(BF16)
            k_t = (k * jnp.exp(jnp.minimum(b_ref - b, 0.0))).astype(BF16)
            blk = jnp.where(col < r0, _dot_nt(q_t, k_t), 0.0)
        for s in range(GLA_SUB):
            e = jnp.exp(jnp.where(sub_row >= s, bd - bd[s:s + 1], -jnp.inf))
            a_col = jnp.sum(qd * e * kd[s:s + 1], axis=-1, keepdims=True)
            blk = jnp.where(col == r0 + s, a_col, blk)
        att_rows.append(blk)
    att = jnp.concatenate(att_rows, axis=0)
    o = _dot(att.astype(BF16), v.astype(BF16))
    o = o + _dot_nt((q * jnp.exp(b)).astype(BF16), s_t.astype(BF16))
    b_last = b[GLA_CHUNK - 1:GLA_CHUNK]
    k_dec = (k * jnp.exp(b_last - b)).astype(BF16)
    s_new = s_t * jnp.exp(b_last) + _dot_tn(v.astype(BF16), k_dec)
    return o, s_new


def _gla_body(h_ref, g2_ref, g3_ref, win_ref, wa_ref, wgu_ref, bg_ref, gn_ref, wout_ref, tri_ref,
              o_ref, s_ref, p_ref, la_ref, oc_ref, *, tb):
    jb = pl.program_id(1)

    @pl.when(jb == 0)
    def _():
        s_ref[...] = jnp.zeros_like(s_ref)

    h = h_ref[0]
    ub = _rms(h, g2_ref[...]).astype(BF16)
    p = _dot(ub, win_ref[...])
    pos = jb * tb + lax.broadcasted_iota(jnp.int32, (tb, 1), 0)
    valid = pos >= PAD
    p_ref[:, :GLA_DK] = p[:, :GLA_DK] * (GLA_HK ** -0.5)
    p_ref[:, GLA_DK:2 * GLA_DK] = jnp.where(valid, p[:, GLA_DK:2 * GLA_DK], 0.0)
    p_ref[:, 2 * GLA_DK:2 * GLA_DK + GLA_DV] = jnp.where(valid, p[:, 2 * GLA_DK:2 * GLA_DK + GLA_DV], 0.0)
    p_ref[:, 2 * GLA_DK + GLA_DV:] = p[:, 2 * GLA_DK + GLA_DV:]
    a = _dot(ub, wa_ref[...]).astype(BF16)
    x = _dot(a, wgu_ref[...]) + bg_ref[...]
    la_ref[...] = _log_sigmoid(x) * (1.0 / GLA_GATE_NORMALIZER)

    def chunk(c, carry):
        r0 = pl.multiple_of(c * GLA_CHUNK, GLA_CHUNK)
        rows = pl.ds(r0, GLA_CHUNK)
        tri = tri_ref[...]
        b_all = sum(_dot(tri, part) for part in _split3(la_ref[rows, :]))
        for hh in range(GLA_HEADS):
            kq = slice(hh * GLA_HK, (hh + 1) * GLA_HK)
            q = p_ref[rows, kq]
            k = p_ref[rows, GLA_DK + hh * GLA_HK:GLA_DK + (hh + 1) * GLA_HK]
            v = p_ref[rows, 2 * GLA_DK + hh * GLA_HV:2 * GLA_DK + (hh + 1) * GLA_HV]
            o, s_new = _gla_chunk_head(q, k, v, b_all[:, kq], s_ref[hh])
            s_ref[hh] = s_new
            oc_ref[rows, hh * GLA_HV:(hh + 1) * GLA_HV] = o
        return carry

    lax.fori_loop(0, tb // GLA_CHUNK, chunk, 0)

    gn = gn_ref[...]
    normed = []
    for hh in range(GLA_HEADS):
        oh = oc_ref[:, hh * GLA_HV:(hh + 1) * GLA_HV]
        normed.append(oh * lax.rsqrt(jnp.mean(oh * oh, axis=-1, keepdims=True) + RMS_EPS) * gn)
    gate = p_ref[:, 2 * GLA_DK + GLA_DV:]
    y = jnp.concatenate(normed, axis=-1) * (gate * _sigmoid(gate))
    z = _dot(y.astype(BF16), wout_ref[...])
    o_ref[0] = h + _rms(z, g3_ref[...])


def _gla_mixer(h3, g_pre, g_post, w_in, w_a, w_gu, b_gate, g_norm, w_out, tri, *, tb):
    bsz, lp, _ = h3.shape
    blk = pl.BlockSpec((1, tb, D_MODEL), lambda b, j: (b, j, 0))
    n_in = 2 * GLA_DK + 2 * GLA_DV
    return pl.pallas_call(
        functools.partial(_gla_body, tb=tb),
        grid=(bsz, lp // tb),
        in_specs=[blk, _const_spec((1, D_MODEL)), _const_spec((1, D_MODEL)),
                  _const_spec((D_MODEL, n_in)), _const_spec((D_MODEL, LANES)),
                  _const_spec((LANES, GLA_DK)), _const_spec((1, GLA_DK)), _const_spec((1, GLA_HV)),
                  _const_spec((GLA_DV, D_MODEL)), _const_spec((GLA_CHUNK, GLA_CHUNK))],
        out_specs=blk,
        out_shape=jax.ShapeDtypeStruct(h3.shape, F32),
        scratch_shapes=[pltpu.VMEM((GLA_HEADS, GLA_HV, GLA_HK), F32),
                        pltpu.VMEM((tb, n_in), F32),
                        pltpu.VMEM((tb, GLA_DK), F32),
                        pltpu.VMEM((tb, GLA_DV), F32)],
        compiler_params=_params(2),
        name="gla_mixer",
    )(h3, g_pre, g_post, w_in, w_a, w_gu, b_gate, g_norm, w_out, tri)


def _proj_in_body(h_ref, g_ref, w_ref, o_ref):
    o_ref[...] = _dot(_rms(h_ref[...], g_ref[...]).astype(BF16), w_ref[...]).astype(o_ref.dtype)


def _proj_in(h, g, w, *, tm):
    t, n = h.shape[0], w.shape[1]
    return pl.pallas_call(
        _proj_in_body,
        grid=(t // tm,),
        in_specs=[pl.BlockSpec((tm, D_MODEL), lambda i: (i, 0)), _const_spec((1, D_MODEL)),
                  _const_spec(w.shape)],
        out_specs=pl.BlockSpec((tm, n), lambda i: (i, 0)),
        out_shape=jax.ShapeDtypeStruct((t, n), BF16),
        compiler_params=_params(1),
        name="proj_in",
    )(h, g, w)


def _proj_out_body(h_ref, x_ref, g_ref, w_ref, o_ref):
    o_ref[...] = h_ref[...] + _rms(_dot(x_ref[...], w_ref[...]), g_ref[...])


def _proj_out(h, x, g, w, *, tm):
    t, kdim = x.shape
    row = pl.BlockSpec((tm, D_MODEL), lambda i: (i, 0))
    return pl.pallas_call(
        _proj_out_body,
        grid=(t // tm,),
        in_specs=[row, pl.BlockSpec((tm, kdim), lambda i: (i, 0)), _const_spec((1, D_MODEL)),
                  _const_spec(w.shape)],
        out_specs=row,
        out_shape=jax.ShapeDtypeStruct(h.shape, F32),
        compiler_params=_params(1),
        name="proj_out",
    )(h, x, g, w)


def _sb_body(q_ref, k_ref, v_ref, u_ref, o_ref, acc_ref, car_ref, *, nq):
    lane = lax.broadcasted_iota(jnp.int32, (1, LANES), 1)
    head_lanes = [lane < SB_HEAD_DIM, lane >= SB_HEAD_DIM]
    row = lax.broadcasted_iota(jnp.int32, (SB_BLOCK, 1), 0)
    col = lax.broadcasted_iota(jnp.int32, (1, SB_BLOCK), 1)
    zero = jnp.zeros((), BF16)

    def qblock(m, carry):
        q0 = pl.multiple_of(m * SB_BLOCK, SB_BLOCK)
        q = q_ref[0, pl.ds(q0, SB_BLOCK), :] * (SB_HEAD_DIM ** -0.5)
        qs = [jnp.where(hl, q, zero) for hl in head_lanes]
        acc_ref[...] = jnp.zeros_like(acc_ref)
        car_ref[...] = jnp.zeros_like(car_ref)

        def step(j, masked):
            k0 = pl.multiple_of(j * SB_BLOCK, SB_BLOCK)
            kb = k_ref[0, pl.ds(k0, SB_BLOCK), :]
            vb = v_ref[0, pl.ds(k0, SB_BLOCK), :]
            if masked:
                s_glob = k0 + col
                mask = (s_glob < q0 + row) & (s_glob >= PAD)
            acc = acc_ref[...]
            for hh in range(2):
                z = _dot_nt(qs[hh], kb)
                ls = jnp.minimum(z, 0.0) - jnp.log1p(jnp.exp(-jnp.abs(z)))
                l1m = ls - z
                if masked:
                    l1m = jnp.where(mask, l1m, 0.0)
                hi = l1m.astype(BF16)
                lo = (l1m - hi.astype(F32)).astype(BF16)
                cs = _dot(jnp.concatenate([hi, lo], axis=1), u_ref[...])
                car = car_ref[hh]
                w = jnp.exp(ls + cs[:, :SB_BLOCK] + car)
                if masked:
                    w = jnp.where(mask, w, 0.0)
                acc = acc + _dot(w.astype(BF16), jnp.where(head_lanes[hh], vb, zero))
                car_ref[hh] = car + cs[:, SB_BLOCK:]
            acc_ref[...] = acc

        step(m, True)

        def full(t, c):
            step(m - 1 - t, False)
            return c

        lax.fori_loop(0, jnp.maximum(m - 1, 0), full, 0)

        @pl.when(m >= 1)
        def _():
            step(0, True)

        o_ref[0, pl.ds(q0, SB_BLOCK), :] = acc_ref[...].astype(o_ref.dtype)
        return carry

    lax.fori_loop(0, nq, qblock, 0)


def _sb_attention(qkv3, u2):
    bsz, lp, _ = qkv3.shape
    n_pairs = D_MODEL // LANES

    def col_spec(offset):
        return pl.BlockSpec((1, lp, LANES), lambda b, p: (b, 0, offset + p))

    return pl.pallas_call(
        functools.partial(_sb_body, nq=lp // SB_BLOCK),
        grid=(bsz, n_pairs),
        in_specs=[col_spec(0), col_spec(n_pairs), col_spec(2 * n_pairs),
                  _const_spec((2 * SB_BLOCK, 2 * SB_BLOCK))],
        out_specs=pl.BlockSpec((1, lp, LANES), lambda b, p: (b, 0, p)),
        out_shape=jax.ShapeDtypeStruct((bsz, lp, D_MODEL), BF16),
        scratch_shapes=[pltpu.VMEM((SB_BLOCK, LANES), F32),
                        pltpu.VMEM((2, SB_BLOCK, SB_BLOCK), F32)],
        compiler_params=_params(2),
        name="sb_attention",
    )(qkv3, qkv3, qkv3, u2)


def _suffix_sum_matrix():
    j = jnp.arange(SB_BLOCK)[:, None]
    s = jnp.arange(2 * SB_BLOCK)[None, :]
    u = ((j > s) | (s >= SB_BLOCK)).astype(BF16)
    return jnp.concatenate([u, u], axis=0)


def kernel(x, meta_tokens, norm_gains, ffn_w_in, ffn_w_out, conv_w_in, conv_w, conv_w_out,
           gla_w_in, gla_w_gate_up, gla_b_gate, gla_norm, gla_w_out, sb_w_in, sb_w_out):
    bsz, seq, _ = x.shape
    depth = norm_gains.shape[0]
    lp = PAD + N_META + seq
    assert lp % SB_BLOCK == 0 and lp % GLA_CHUNK == 0
    t = bsz * lp
    tm = _row_tile(t, (512, 384, 256, 128))
    tb = _row_tile(lp, (384, 256, 128, 64))

    meta = jnp.broadcast_to(meta_tokens[None].astype(x.dtype), (bsz, N_META, D_MODEL))
    h = jnp.concatenate([jnp.zeros((bsz, PAD, D_MODEL), x.dtype), meta, x], axis=1).reshape(t, D_MODEL)

    gains = norm_gains.reshape(depth, -1, 1, D_MODEL)
    ffn_w_in, ffn_w_out = ffn_w_in.astype(BF16), ffn_w_out.astype(BF16)
    tri = (jnp.arange(GLA_CHUNK)[:, None] >= jnp.arange(GLA_CHUNK)[None, :]).astype(BF16)
    u2 = _suffix_sum_matrix()

    for i in range(depth):
        g = gains[i]
        h = _ffn(h, g[0], g[1], ffn_w_in[i, 0], ffn_w_out[i, 0], tm=tm)
        kind, j = i % 3, i // 3
        if kind == 0:
            cw = jnp.pad(conv_w[j], ((0, 8 - CONV_WIDTH), (0, 0)))
            h = _conv_mixer(h, g[2], g[3], cw, conv_w_in[j].astype(BF16), conv_w_out[j].astype(BF16),
                            tm=tm, lp=lp)
        elif kind == 1:
            n_main = 2 * GLA_DK + 2 * GLA_DV
            w_a = jnp.pad(gla_w_in[j][:, n_main:], ((0, 0), (0, LANES - GLA_RANK))).astype(BF16)
            w_gu = jnp.pad(gla_w_gate_up[j], ((0, LANES - GLA_RANK), (0, 0))).astype(BF16)
            h = _gla_mixer(h.reshape(bsz, lp, D_MODEL), g[2], g[3], gla_w_in[j][:, :n_main].astype(BF16),
                           w_a, w_gu, gla_b_gate[j][None], gla_norm[j][None], gla_w_out[j].astype(BF16),
                           tri, tb=tb).reshape(t, D_MODEL)
        else:
            qkv = _proj_in(h, g[2], sb_w_in[j].astype(BF16), tm=tm)
            o = _sb_attention(qkv.reshape(bsz, lp, 3 * D_MODEL), u2)
            h = _proj_out(h, o.reshape(t, D_MODEL), g[3], sb_w_out[j].astype(BF16), tm=tm)
        h = _ffn(h, g[4], g[5], ffn_w_in[i, 1], ffn_w_out[i, 1], tm=tm)
    return h.reshape(bsz, lp, D_MODEL)[:, PAD + N_META:]
```

```python
import functools

import jax
import jax.numpy as jnp
from jax import lax
from jax.experimental import pallas as pl
from jax.experimental.pallas import tpu as pltpu

F32 = jnp.float32
BF16 = jnp.bfloat16

D_MODEL = 1024
D_FF = 2816
N_META = 16
RMS_EPS = 1e-6
CONV_WIDTH = 3
GLA_HEADS = 4
GLA_HK = 128
GLA_HV = 256
GLA_DK = GLA_HEADS * GLA_HK
GLA_DV = GLA_HEADS * GLA_HV
GLA_RANK = 16
GLA_GATE_NORMALIZER = 16.0
GLA_CHUNK = 64
GLA_SUB = 16
SB_HEAD_DIM = 64
SB_BLOCK = 128
LANES = 128
PAD = SB_BLOCK - N_META

VMEM_LIMIT_BYTES = 56 * 1024 * 1024


def _row_tile(total, candidates=(1024, 512, 384, 256, 128)):
    for c in candidates:
        if total % c == 0:
            return c
    raise ValueError(f"no row tile for {total}")


def _rms(x, g):
    return x * lax.rsqrt(jnp.mean(x * x, axis=-1, keepdims=True) + RMS_EPS) * g


def _dot(a, b):
    return jnp.dot(a, b, preferred_element_type=F32)


def _dot_nt(a, b):
    return lax.dot_general(a, b, (((1,), (1,)), ((), ())), preferred_element_type=F32)


def _dot_tn(a, b):
    return lax.dot_general(a, b, (((0,), (0,)), ((), ())), preferred_element_type=F32)


def _sigmoid(x):
    return 1.0 / (1.0 + jnp.exp(-x))


def _log_sigmoid(x):
    return jnp.minimum(x, 0.0) - jnp.log1p(jnp.exp(-jnp.abs(x)))


def _const_spec(shape):
    return pl.BlockSpec(shape, lambda *_: (0,) * len(shape), pipeline_mode=pl.Buffered(1))


def _params(n_axes):
    return pltpu.CompilerParams(dimension_semantics=("arbitrary",) * n_axes,
                                vmem_limit_bytes=VMEM_LIMIT_BYTES)


def _ffn_body(h_ref, g0_ref, g1_ref, win_ref, wout_ref, o_ref, *, ff_chunk):
    h = h_ref[...]
    xb = _rms(h, g0_ref[...]).astype(BF16)
    y = None
    for c in range(D_FF // ff_chunk):
        lo = c * ff_chunk
        gate = _dot(xb, win_ref[:, lo:lo + ff_chunk])
        up = _dot(xb, win_ref[:, D_FF + lo:D_FF + lo + ff_chunk])
        a = (gate * _sigmoid(gate) * up).astype(BF16)
        part = _dot(a, wout_ref[lo:lo + ff_chunk, :])
        y = part if y is None else y + part
    o_ref[...] = h + 0.5 * _rms(y, g1_ref[...])


def _ffn(h, g_pre, g_post, w_in, w_out, *, tm, ff_chunk=D_FF):
    t = h.shape[0]
    row = pl.BlockSpec((tm, D_MODEL), lambda i: (i, 0))
    return pl.pallas_call(
        functools.partial(_ffn_body, ff_chunk=ff_chunk),
        grid=(t // tm,),
        in_specs=[row, _const_spec((1, D_MODEL)), _const_spec((1, D_MODEL)),
                  _const_spec((D_MODEL, 2 * D_FF)), _const_spec((D_FF, D_MODEL))],
        out_specs=row,
        out_shape=jax.ShapeDtypeStruct(h.shape, F32),
        compiler_params=_params(1),
        name="ffn",
    )(h, g_pre, g_post, w_in, w_out)


def _conv_body(h_ref, halo_ref, g2_ref, g3_ref, cw_ref, win_ref, wout_ref, o_ref, *, tm, lp):
    i = pl.program_id(0)
    h = h_ref[...]
    g2 = g2_ref[...]
    xb = _rms(h, g2).astype(BF16)
    p = _dot(xb, win_ref[...])
    row = lax.broadcasted_iota(jnp.int32, (tm, 1), 0)
    pos = (i * tm + row) % lp
    uu = jnp.where(pos >= PAD, p[:, D_MODEL:2 * D_MODEL] * p[:, 2 * D_MODEL:], 0.0)
    xh = _rms(halo_ref[...], g2).astype(BF16)
    ph = _dot(xh, win_ref[:, D_MODEL:])
    hrow = lax.broadcasted_iota(jnp.int32, (8, 1), 0)
    hpos = (i * tm - 8 + hrow + lp) % lp
    uh = jnp.where(hpos >= PAD, ph[:, :D_MODEL] * ph[:, D_MODEL:], 0.0)
    prev1 = jnp.where(row == 0, uh[7:8], pltpu.roll(uu, 1, 0))
    prev2 = jnp.where(row == 0, uh[6:7], jnp.where(row == 1, uh[7:8], pltpu.roll(uu, 2, 0)))
    prev1 = jnp.where(pos >= PAD + 1, prev1, 0.0)
    prev2 = jnp.where(pos >= PAD + 2, prev2, 0.0)
    cw = cw_ref[...]
    conv = cw[2:3] * uu + cw[1:2] * prev1 + cw[0:1] * prev2
    y = _dot((p[:, :D_MODEL] * conv).astype(BF16), wout_ref[...])
    o_ref[...] = h + _rms(y, g3_ref[...])


def _conv_mixer(h, g_pre, g_post, conv_w, w_in, w_out, *, tm, lp):
    t = h.shape[0]
    row = pl.BlockSpec((tm, D_MODEL), lambda i: (i, 0))
    halo = pl.BlockSpec((8, D_MODEL), lambda i: (jnp.maximum(i * (tm // 8) - 1, 0), 0))
    return pl.pallas_call(
        functools.partial(_conv_body, tm=tm, lp=lp),
        grid=(t // tm,),
        in_specs=[row, halo, _const_spec((1, D_MODEL)), _const_spec((1, D_MODEL)),
                  _const_spec((8, D_MODEL)), _const_spec((D_MODEL, 3 * D_MODEL)),
                  _const_spec((D_MODEL, D_MODEL))],
        out_specs=row,
        out_shape=jax.ShapeDtypeStruct(h.shape, F32),
        compiler_params=_params(1),
        name="conv_mixer",
    )(h, h, g_pre, g_post, conv_w, w_in, w_out)


def _split3(x):
    hi = x.astype(BF16)
    r = x - hi.astype(F32)
    mid = r.astype(BF16)
    lo = (r - mid.astype(F32)).astype(BF16)
    return hi, mid, lo


def _gla_chunk_head(q, k, v, b, s_t):
    n_sub = GLA_CHUNK // GLA_SUB
    sub_row = lax.broadcasted_iota(jnp.int32, (GLA_SUB, 1), 0)
    col = lax.broadcasted_iota(jnp.int32, (1, GLA_CHUNK), 1)
    att_rows = []
    for i in range(n_sub):
        r0 = i * GLA_SUB
        qd, kd, bd = q[r0:r0 + GLA_SUB], k[r0:r0 + GLA_SUB], b[r0:r0 + GLA_SUB]
        blk = jnp.zeros((GLA_SUB, GLA_CHUNK), F32)
        if i > 0:
            b_ref = b[r0 - 1:r0]
            q_t = (qd * jnp.exp(bd - b_ref)).astype(BF16)
            k_t = (k * jnp.exp(jnp.minimum(b_ref - b, 0.0))).astype(BF16)
            blk = jnp.where(col < r0, _dot_nt(q_t, k_t), 0.0)
        for s in range(GLA_SUB):
            e = jnp.exp(jnp.where(sub_row >= s, bd - bd[s:s + 1], -jnp.inf))
            a_col = jnp.sum(qd * e * kd[s:s + 1], axis=-1, keepdims=True)
            blk = jnp.where(col == r0 + s, a_col, blk)
        att_rows.append(blk)
    att = jnp.concatenate(att_rows, axis=0)
    o = _dot(att.astype(BF16), v.astype(BF16))
    o = o + _dot_nt((q * jnp.exp(b)).astype(BF16), s_t.astype(BF16))
    b_last = b[GLA_CHUNK - 1:GLA_CHUNK]
    k_dec = (k * jnp.exp(b_last - b)).astype(BF16)
    s_new = s_t * jnp.exp(b_last) + _dot_tn(v.astype(BF16), k_dec)
    return o, s_new


def _gla_body(h_ref, g2_ref, g3_ref, win_ref, wa_ref, wgu_ref, bg_ref, gn_ref, wout_ref, tri_ref,
              o_ref, s_ref, p_ref, la_ref, oc_ref, *, tb):
    jb = pl.program_id(1)

    @pl.when(jb == 0)
    def _():
        s_ref[...] = jnp.zeros_like(s_ref)

    h = h_ref[0]
    ub = _rms(h, g2_ref[...]).astype(BF16)
    p = _dot(ub, win_ref[...])
    pos = jb * tb + lax.broadcasted_iota(jnp.int32, (tb, 1), 0)
    valid = pos >= PAD
    p_ref[:, :GLA_DK] = p[:, :GLA_DK] * (GLA_HK ** -0.5)
    p_ref[:, GLA_DK:2 * GLA_DK] = jnp.where(valid, p[:, GLA_DK:2 * GLA_DK], 0.0)
    p_ref[:, 2 * GLA_DK:2 * GLA_DK + GLA_DV] = jnp.where(valid, p[:, 2 * GLA_DK:2 * GLA_DK + GLA_DV], 0.0)
    p_ref[:, 2 * GLA_DK + GLA_DV:] = p[:, 2 * GLA_DK + GLA_DV:]
    a = _dot(ub, wa_ref[...]).astype(BF16)
    x = _dot(a, wgu_ref[...]) + bg_ref[...]
    la_ref[...] = _log_sigmoid(x) * (1.0 / GLA_GATE_NORMALIZER)

    def chunk(c, carry):
        r0 = pl.multiple_of(c * GLA_CHUNK, GLA_CHUNK)
        rows = pl.ds(r0, GLA_CHUNK)
        tri = tri_ref[...]
        b_all = sum(_dot(tri, part) for part in _split3(la_ref[rows, :]))
        for hh in range(GLA_HEADS):
            kq = slice(hh * GLA_HK, (hh + 1) * GLA_HK)
            q = p_ref[rows, kq]
            k = p_ref[rows, GLA_DK + hh * GLA_HK:GLA_DK + (hh + 1) * GLA_HK]
            v = p_ref[rows, 2 * GLA_DK + hh * GLA_HV:2 * GLA_DK + (hh + 1) * GLA_HV]
            o, s_new = _gla_chunk_head(q, k, v, b_all[:, kq], s_ref[hh])
            s_ref[hh] = s_new
            oc_ref[rows, hh * GLA_HV:(hh + 1) * GLA_HV] = o
        return carry

    lax.fori_loop(0, tb // GLA_CHUNK, chunk, 0)

    gn = gn_ref[...]
    normed = []
    for hh in range(GLA_HEADS):
        oh = oc_ref[:, hh * GLA_HV:(hh + 1) * GLA_HV]
        normed.append(oh * lax.rsqrt(jnp.mean(oh * oh, axis=-1, keepdims=True) + RMS_EPS) * gn)
    gate = p_ref[:, 2 * GLA_DK + GLA_DV:]
    y = jnp.concatenate(normed, axis=-1) * (gate * _sigmoid(gate))
    z = _dot(y.astype(BF16), wout_ref[...])
    o_ref[0] = h + _rms(z, g3_ref[...])


def _gla_mixer(h3, g_pre, g_post, w_in, w_a, w_gu, b_gate, g_norm, w_out, tri, *, tb):
    bsz, lp, _ = h3.shape
    blk = pl.BlockSpec((1, tb, D_MODEL), lambda b, j: (b, j, 0))
    n_in = 2 * GLA_DK + 2 * GLA_DV
    return pl.pallas_call(
        functools.partial(_gla_body, tb=tb),
        grid=(bsz, lp // tb),
        in_specs=[blk, _const_spec((1, D_MODEL)), _const_spec((1, D_MODEL)),
                  _const_spec((D_MODEL, n_in)), _const_spec((D_MODEL, LANES)),
                  _const_spec((LANES, GLA_DK)), _const_spec((1, GLA_DK)), _const_spec((1, GLA_HV)),
                  _const_spec((GLA_DV, D_MODEL)), _const_spec((GLA_CHUNK, GLA_CHUNK))],
        out_specs=blk,
        out_shape=jax.ShapeDtypeStruct(h3.shape, F32),
        scratch_shapes=[pltpu.VMEM((GLA_HEADS, GLA_HV, GLA_HK), F32),
                        pltpu.VMEM((tb, n_in), F32),
                        pltpu.VMEM((tb, GLA_DK), F32),
                        pltpu.VMEM((tb, GLA_DV), F32)],
        compiler_params=_params(2),
        name="gla_mixer",
    )(h3, g_pre, g_post, w_in, w_a, w_gu, b_gate, g_norm, w_out, tri)


def _proj_in_body(h_ref, g_ref, w_ref, o_ref):
    o_ref[...] = _dot(_rms(h_ref[...], g_ref[...]).astype(BF16), w_ref[...]).astype(o_ref.dtype)


def _proj_in(h, g, w, *, tm):
    t, n = h.shape[0], w.shape[1]
    return pl.pallas_call(
        _proj_in_body,
        grid=(t // tm,),
        in_specs=[pl.BlockSpec((tm, D_MODEL), lambda i: (i, 0)), _const_spec((1, D_MODEL)),
                  _const_spec(w.shape)],
        out_specs=pl.BlockSpec((tm, n), lambda i: (i, 0)),
        out_shape=jax.ShapeDtypeStruct((t, n), BF16),
        compiler_params=_params(1),
        name="proj_in",
    )(h, g, w)


def _proj_out_body(h_ref, x_ref, g_ref, w_ref, o_ref):
    o_ref[...] = h_ref[...] + _rms(_dot(x_ref[...], w_ref[...]), g_ref[...])


def _proj_out(h, x, g, w, *, tm):
    t, kdim = x.shape
    row = pl.BlockSpec((tm, D_MODEL), lambda i: (i, 0))
    return pl.pallas_call(
        _proj_out_body,
        grid=(t // tm,),
        in_specs=[row, pl.BlockSpec((tm, kdim), lambda i: (i, 0)), _const_spec((1, D_MODEL)),
                  _const_spec(w.shape)],
        out_specs=row,
        out_shape=jax.ShapeDtypeStruct(h.shape, F32),
        compiler_params=_params(1),
        name="proj_out",
    )(h, x, g, w)


SB_WIDE = 3
SB_DEAD = -104.0


def _sb_scores(qs, kwin, mask):
    z = _dot_nt(qs, kwin)
    ls = jnp.minimum(z, 0.0) - jnp.log1p(jnp.exp(-jnp.abs(z)))
    return ls, jnp.where(mask, ls - z, 0.0)


def _sb_suffix(l1m, u):
    hi = l1m.astype(BF16)
    lo = (l1m - hi.astype(F32)).astype(BF16)
    return _dot(jnp.concatenate([hi, lo], axis=1), u)


def _sb_body(q_ref, k_ref, v_ref, u_ref, o_ref, *, nq):
    lane = lax.broadcasted_iota(jnp.int32, (1, LANES), 1)
    low = lane < SB_HEAD_DIM
    zero = jnp.zeros((), BF16)
    row = lax.broadcasted_iota(jnp.int32, (2 * SB_BLOCK, 1), 0) % SB_BLOCK
    col = lax.broadcasted_iota(jnp.int32, (1, SB_BLOCK), 1)
    u = u_ref[...]

    def qblock(m, carry):
        q0 = pl.multiple_of(m * SB_BLOCK, SB_BLOCK)
        q = q_ref[0, pl.ds(q0, SB_BLOCK), :] * (SB_HEAD_DIM ** -0.5)
        qs = jnp.concatenate([jnp.where(low, q, zero), jnp.where(low, zero, q)], axis=0)
        t_glob = q0 + row

        kparts, vparts, masks = [], [], []
        for bi in range(SB_WIDE):
            jv = m - (SB_WIDE - 1) + bi
            k0 = pl.multiple_of(jnp.maximum(jv, 0) * SB_BLOCK, SB_BLOCK)
            kparts.append(k_ref[0, pl.ds(k0, SB_BLOCK), :])
            vparts.append(v_ref[0, pl.ds(k0, SB_BLOCK), :])
            s_glob = jv * SB_BLOCK + col
            real = jnp.broadcast_to(s_glob >= PAD, (2 * SB_BLOCK, SB_BLOCK))
            masks.append((s_glob < t_glob) & real if bi == SB_WIDE - 1 else real)
        mask = jnp.concatenate(masks, axis=1)
        ls, l1m = _sb_scores(qs, jnp.concatenate(kparts, axis=0), mask)
        cs = [_sb_suffix(l1m[:, bi * SB_BLOCK:(bi + 1) * SB_BLOCK], u) for bi in range(SB_WIDE)]
        tails, later = [], None
        for bi in reversed(range(SB_WIDE)):
            tail = cs[bi][:, :SB_BLOCK]
            tails.append(tail if later is None else tail + later)
            tot = cs[bi][:, SB_BLOCK:]
            later = tot if later is None else later + tot
        tail = jnp.concatenate(tails[::-1], axis=1)
        w = jnp.where(mask, jnp.exp(ls + tail), 0.0)
        acc = _dot(w.astype(BF16), jnp.concatenate(vparts, axis=0))
        car = later

        def alive(c):
            return (jnp.max(c) >= SB_DEAD).astype(jnp.int32)

        def cond(st):
            return (st[0] >= 0) & (st[1] > 0)

        def body(st):
            j, _, car, acc = st
            k0 = pl.multiple_of(j * SB_BLOCK, SB_BLOCK)
            mask = jnp.broadcast_to(k0 + col >= PAD, (2 * SB_BLOCK, SB_BLOCK))
            ls, l1m = _sb_scores(qs, k_ref[0, pl.ds(k0, SB_BLOCK), :], mask)
            cs = _sb_suffix(l1m, u)
            w = jnp.where(mask, jnp.exp(ls + cs[:, :SB_BLOCK] + car), 0.0)
            acc = acc + _dot(w.astype(BF16), v_ref[0, pl.ds(k0, SB_BLOCK), :])
            car = car + cs[:, SB_BLOCK:]
            return j - 1, alive(car), car, acc

        _, _, _, acc = lax.while_loop(cond, body, (m - SB_WIDE, alive(car), car, acc))
        out = jnp.where(low, acc[:SB_BLOCK], acc[SB_BLOCK:])
        o_ref[0, pl.ds(q0, SB_BLOCK), :] = out.astype(o_ref.dtype)
        return carry

    lax.fori_loop(0, nq, qblock, 0)


def _sb_attention(qkv3, u2):
    bsz, lp, _ = qkv3.shape
    n_pairs = D_MODEL // LANES
    assert lp >= SB_WIDE * SB_BLOCK

    def col_spec(offset):
        return pl.BlockSpec((1, lp, LANES), lambda b, p: (b, 0, offset + p))

    return pl.pallas_call(
        functools.partial(_sb_body, nq=lp // SB_BLOCK),
        grid=(bsz, n_pairs),
        in_specs=[col_spec(0), col_spec(n_pairs), col_spec(2 * n_pairs),
                  _const_spec((2 * SB_BLOCK, 2 * SB_BLOCK))],
        out_specs=pl.BlockSpec((1, lp, LANES), lambda b, p: (b, 0, p)),
        out_shape=jax.ShapeDtypeStruct((bsz, lp, D_MODEL), BF16),
        compiler_params=_params(2),
        name="sb_attention",
    )(qkv3, qkv3, qkv3, u2)


def _suffix_sum_matrix():
    j = jnp.arange(SB_BLOCK)[:, None]
    s = jnp.arange(2 * SB_BLOCK)[None, :]
    u = ((j > s) | (s >= SB_BLOCK)).astype(BF16)
    return jnp.concatenate([u, u], axis=0)


def kernel(x, meta_tokens, norm_gains, ffn_w_in, ffn_w_out, conv_w_in, conv_w, conv_w_out,
           gla_w_in, gla_w_gate_up, gla_b_gate, gla_norm, gla_w_out, sb_w_in, sb_w_out):
    bsz, seq, _ = x.shape
    depth = norm_gains.shape[0]
    lp = PAD + N_META + seq
    assert lp % SB_BLOCK == 0 and lp % GLA_CHUNK == 0
    t = bsz * lp
    tm = _row_tile(t, (512, 384, 256, 128))
    tb = _row_tile(lp, (384, 256, 128, 64))

    meta = jnp.broadcast_to(meta_tokens[None].astype(x.dtype), (bsz, N_META, D_MODEL))
    h = jnp.concatenate([jnp.zeros((bsz, PAD, D_MODEL), x.dtype), meta, x], axis=1).reshape(t, D_MODEL)

    gains = norm_gains.reshape(depth, -1, 1, D_MODEL)
    ffn_w_in, ffn_w_out = ffn_w_in.astype(BF16), ffn_w_out.astype(BF16)
    tri = (jnp.arange(GLA_CHUNK)[:, None] >= jnp.arange(GLA_CHUNK)[None, :]).astype(BF16)
    u2 = _suffix_sum_matrix()

    for i in range(depth):
        g = gains[i]
        h = _ffn(h, g[0], g[1], ffn_w_in[i, 0], ffn_w_out[i, 0], tm=tm)
        kind, j = i % 3, i // 3
        if kind == 0:
            cw = jnp.pad(conv_w[j], ((0, 8 - CONV_WIDTH), (0, 0)))
            h = _conv_mixer(h, g[2], g[3], cw, conv_w_in[j].astype(BF16), conv_w_out[j].astype(BF16),
                            tm=tm, lp=lp)
        elif kind == 1:
            n_main = 2 * GLA_DK + 2 * GLA_DV
            w_a = jnp.pad(gla_w_in[j][:, n_main:], ((0, 0), (0, LANES - GLA_RANK))).astype(BF16)
            w_gu = jnp.pad(gla_w_gate_up[j], ((0, LANES - GLA_RANK), (0, 0))).astype(BF16)
            h = _gla_mixer(h.reshape(bsz, lp, D_MODEL), g[2], g[3], gla_w_in[j][:, :n_main].astype(BF16),
                           w_a, w_gu, gla_b_gate[j][None], gla_norm[j][None], gla_w_out[j].astype(BF16),
                           tri, tb=tb).reshape(t, D_MODEL)
        else:
            qkv = _proj_in(h, g[2], sb_w_in[j].astype(BF16), tm=tm)
            o = _sb_attention(qkv.reshape(bsz, lp, 3 * D_MODEL), u2)
            h = _proj_out(h, o.reshape(t, D_MODEL), g[3], sb_w_out[j].astype(BF16), tm=tm)
        h = _ffn(h, g[4], g[5], ffn_w_in[i, 1], ffn_w_out[i, 1], tm=tm)
    return h.reshape(bsz, lp, D_MODEL)[:, PAD + N_META:]
```

```python
import functools

import jax
import jax.numpy as jnp
from jax import lax
from jax.experimental import pallas as pl
from jax.experimental.pallas import tpu as pltpu

F32 = jnp.float32
BF16 = jnp.bfloat16

D_MODEL = 1024
D_FF = 2816
N_META = 16
RMS_EPS = 1e-6
CONV_WIDTH = 3
GLA_HEADS = 4
GLA_HK = 128
GLA_HV = 256
GLA_DK = GLA_HEADS * GLA_HK
GLA_DV = GLA_HEADS * GLA_HV
GLA_RANK = 16
GLA_GATE_NORMALIZER = 16.0
GLA_CHUNK = 64
GLA_SUB = 16
GLA_MILD_DECAY = 40.0
SB_HEAD_DIM = 64
SB_BLOCK = 128
LANES = 128
PAD = SB_BLOCK - N_META

VMEM_LIMIT_BYTES = 56 * 1024 * 1024


def _row_tile(total, candidates=(1024, 512, 384, 256, 128)):
    for c in candidates:
        if total % c == 0:
            return c
    raise ValueError(f"no row tile for {total}")


def _rms(x, g):
    return x * lax.rsqrt(jnp.mean(x * x, axis=-1, keepdims=True) + RMS_EPS) * g


def _dot(a, b):
    return jnp.dot(a, b, preferred_element_type=F32)


def _dot_nt(a, b):
    return lax.dot_general(a, b, (((1,), (1,)), ((), ())), preferred_element_type=F32)


def _dot_tn(a, b):
    return lax.dot_general(a, b, (((0,), (0,)), ((), ())), preferred_element_type=F32)


def _sigmoid(x):
    return 1.0 / (1.0 + jnp.exp(-x))


def _log_sigmoid(x):
    return jnp.minimum(x, 0.0) - jnp.log(1.0 + jnp.exp(-jnp.abs(x)))


def _const_spec(shape):
    return pl.BlockSpec(shape, lambda *_: (0,) * len(shape), pipeline_mode=pl.Buffered(1))


def _params(n_axes):
    return pltpu.CompilerParams(dimension_semantics=("arbitrary",) * n_axes,
                                vmem_limit_bytes=VMEM_LIMIT_BYTES)


def _ffn_body(*refs, n_pieces, pieces_per_seq):
    pieces, rest = refs[:n_pieces], refs[n_pieces:]
    head_ref = None
    if pieces_per_seq is not None:
        head_ref, rest = rest[0], rest[1:]
    g0_ref, g1_ref, win_ref, wout_ref, o_ref = rest
    parts = []
    for r, piece in enumerate(pieces):
        part = piece[...]
        if head_ref is not None:
            is_head = (pl.program_id(0) * n_pieces + r) % pieces_per_seq == 0
            part = jnp.where(is_head, head_ref[...], part)
        parts.append(part)
    h = parts[0] if n_pieces == 1 else jnp.concatenate(parts, axis=0)
    xb = _rms(h, g0_ref[...]).astype(BF16)
    gate = _dot(xb, win_ref[:, :D_FF])
    up = _dot(xb, win_ref[:, D_FF:])
    a = (gate * _sigmoid(gate) * up).astype(BF16)
    o_ref[...] = h + 0.5 * _rms(_dot(a, wout_ref[...]), g1_ref[...])


def _ffn(src, g_pre, g_post, w_in, w_out, layer, which, *, tm, mode="stream", seq_head=None, pieces_per_seq=None):
    weights = [pl.BlockSpec((None, None, D_MODEL, 2 * D_FF), lambda i: (layer, which, 0, 0),
                            pipeline_mode=pl.Buffered(1)),
               pl.BlockSpec((None, None, D_FF, D_MODEL), lambda i: (layer, which, 0, 0),
                            pipeline_mode=pl.Buffered(1))]
    gains = [_const_spec((1, D_MODEL)), _const_spec((1, D_MODEL))]
    if mode == "stream":
        n, out_rows = 1, src.shape[0]
        pieces, extra, extra_specs = [pl.BlockSpec((tm, D_MODEL), lambda i: (i, 0))], [], []
    else:
        n = tm // SB_BLOCK
        x_per_seq = pieces_per_seq - 1
        if mode == "enter":
            index = lambda i, r: jnp.maximum(i * n + r - (i * n + r) // pieces_per_seq - 1, 0)
            out_rows = src.shape[0] // x_per_seq * pieces_per_seq
            extra, extra_specs = [seq_head], [_const_spec((SB_BLOCK, D_MODEL))]
        else:
            index = lambda i, r: i * n + r + (i * n + r) // x_per_seq + 1
            out_rows = src.shape[0] // pieces_per_seq * x_per_seq
            extra, extra_specs = [], []
        pieces = [pl.BlockSpec((SB_BLOCK, D_MODEL), functools.partial(lambda i, r: (index(i, r), 0), r=r))
                  for r in range(n)]
    return pl.pallas_call(
        functools.partial(_ffn_body, n_pieces=n, pieces_per_seq=pieces_per_seq if mode == "enter" else None),
        grid=(out_rows // tm,),
        in_specs=pieces + extra_specs + gains + weights,
        out_specs=pl.BlockSpec((tm, D_MODEL), lambda i: (i, 0)),
        out_shape=jax.ShapeDtypeStruct((out_rows, D_MODEL), F32),
        compiler_params=_params(1),
        name="ffn",
    )(*([src] * n), *extra, g_pre, g_post, w_in, w_out)


def _conv_body(h_ref, halo_ref, g2_ref, g3_ref, cw_ref, win_ref, wout_ref, o_ref, *, tm, lp):
    i = pl.program_id(0)
    h = h_ref[...]
    g2 = g2_ref[...]
    xb = _rms(h, g2).astype(BF16)
    p = _dot(xb, win_ref[...])
    row = lax.broadcasted_iota(jnp.int32, (tm, 1), 0)
    pos = (i * tm + row) % lp
    uu = jnp.where(pos >= PAD, p[:, D_MODEL:2 * D_MODEL] * p[:, 2 * D_MODEL:], 0.0)
    xh = _rms(halo_ref[...], g2).astype(BF16)
    ph = _dot(xh, win_ref[:, D_MODEL:])
    hrow = lax.broadcasted_iota(jnp.int32, (8, 1), 0)
    hpos = (i * tm - 8 + hrow + lp) % lp
    uh = jnp.where(hpos >= PAD, ph[:, :D_MODEL] * ph[:, D_MODEL:], 0.0)
    prev1 = jnp.where(row == 0, uh[7:8], pltpu.roll(uu, 1, 0))
    prev2 = jnp.where(row == 0, uh[6:7], jnp.where(row == 1, uh[7:8], pltpu.roll(uu, 2, 0)))
    prev1 = jnp.where(pos >= PAD + 1, prev1, 0.0)
    prev2 = jnp.where(pos >= PAD + 2, prev2, 0.0)
    cw = cw_ref[...]
    conv = cw[2:3] * uu + cw[1:2] * prev1 + cw[0:1] * prev2
    y = _dot((p[:, :D_MODEL] * conv).astype(BF16), wout_ref[...])
    o_ref[...] = h + _rms(y, g3_ref[...])


def _conv_mixer(h, g_pre, g_post, conv_w, w_in, w_out, *, tm, lp):
    t = h.shape[0]
    row = pl.BlockSpec((tm, D_MODEL), lambda i: (i, 0))
    halo = pl.BlockSpec((8, D_MODEL), lambda i: (jnp.maximum(i * (tm // 8) - 1, 0), 0))
    return pl.pallas_call(
        functools.partial(_conv_body, tm=tm, lp=lp),
        grid=(t // tm,),
        in_specs=[row, halo, _const_spec((1, D_MODEL)), _const_spec((1, D_MODEL)),
                  _const_spec((8, D_MODEL)), _const_spec((D_MODEL, 3 * D_MODEL)),
                  _const_spec((D_MODEL, D_MODEL))],
        out_specs=row,
        out_shape=jax.ShapeDtypeStruct(h.shape, F32),
        compiler_params=_params(1),
        name="conv_mixer",
    )(h, h, g_pre, g_post, conv_w, w_in, w_out)


def _split3(x):
    hi = x.astype(BF16)
    r = x - hi.astype(F32)
    mid = r.astype(BF16)
    lo = (r - mid.astype(F32)).astype(BF16)
    return hi, mid, lo


def _gla_chunk_head(q, k, v, b, s_t):
    n_sub = GLA_CHUNK // GLA_SUB
    sub_row = lax.broadcasted_iota(jnp.int32, (GLA_SUB, 1), 0)
    col = lax.broadcasted_iota(jnp.int32, (1, GLA_CHUNK), 1)
    att_rows = []
    for i in range(n_sub):
        r0 = i * GLA_SUB
        qd, kd, bd = q[r0:r0 + GLA_SUB], k[r0:r0 + GLA_SUB], b[r0:r0 + GLA_SUB]
        blk = jnp.zeros((GLA_SUB, GLA_CHUNK), F32)
        if i > 0:
            b_ref = b[r0 - 1:r0]
            q_t = (qd * jnp.exp(bd - b_ref)).astype(BF16)
            k_t = (k * jnp.exp(jnp.minimum(b_ref - b, 0.0))).astype(BF16)
            blk = jnp.where(col < r0, _dot_nt(q_t, k_t), 0.0)
        for s in range(GLA_SUB):
            e = jnp.exp(jnp.where(sub_row >= s, bd - bd[s:s + 1], -jnp.inf))
            a_col = jnp.sum(qd * e * kd[s:s + 1], axis=-1, keepdims=True)
            blk = jnp.where(col == r0 + s, a_col, blk)
        att_rows.append(blk)
    att = jnp.concatenate(att_rows, axis=0)
    o = _dot(att.astype(BF16), v.astype(BF16))
    o = o + _dot_nt((q * jnp.exp(b)).astype(BF16), s_t.astype(BF16))
    b_last = b[GLA_CHUNK - 1:GLA_CHUNK]
    k_dec = (k * jnp.exp(b_last - b)).astype(BF16)
    s_new = s_t * jnp.exp(b_last) + _dot_tn(v.astype(BF16), k_dec)
    return o, s_new


def _gla_chunk_head_mild(q, k, v, b, s_t):
    t_row = lax.broadcasted_iota(jnp.int32, (GLA_CHUNK, 1), 0)
    s_col = lax.broadcasted_iota(jnp.int32, (1, GLA_CHUNK), 1)
    q_dec = (q * jnp.exp(b)).astype(BF16)
    att = jnp.where(s_col <= t_row, _dot_nt(q_dec, (k * jnp.exp(-b)).astype(BF16)), 0.0)
    o = _dot(att.astype(BF16), v.astype(BF16)) + _dot_nt(q_dec, s_t.astype(BF16))
    b_last = b[GLA_CHUNK - 1:GLA_CHUNK]
    k_dec = (k * jnp.exp(b_last - b)).astype(BF16)
    s_new = s_t * jnp.exp(b_last) + _dot_tn(v.astype(BF16), k_dec)
    return o, s_new


def _gla_body(h_ref, g2_ref, g3_ref, win_ref, wa_ref, wgu_ref, bg_ref, gn_ref, wout_ref, tri_ref,
              o_ref, s_ref, p_ref, b_ref, oc_ref, mild_ref, *, tb):
    jb = pl.program_id(1)

    @pl.when(jb == 0)
    def _():
        s_ref[...] = jnp.zeros_like(s_ref)

    h = h_ref[0]
    ub = _rms(h, g2_ref[...]).astype(BF16)
    p = _dot(ub, win_ref[...])
    pos = jb * tb + lax.broadcasted_iota(jnp.int32, (tb, 1), 0)
    valid = pos >= PAD
    p_ref[:, :GLA_DK] = p[:, :GLA_DK] * (GLA_HK ** -0.5)
    p_ref[:, GLA_DK:2 * GLA_DK] = jnp.where(valid, p[:, GLA_DK:2 * GLA_DK], 0.0)
    p_ref[:, 2 * GLA_DK:2 * GLA_DK + GLA_DV] = jnp.where(valid, p[:, 2 * GLA_DK:2 * GLA_DK + GLA_DV], 0.0)
    p_ref[:, 2 * GLA_DK + GLA_DV:] = p[:, 2 * GLA_DK + GLA_DV:]
    a = _dot(ub, wa_ref[...]).astype(BF16)
    x = _dot(a, wgu_ref[...]) + bg_ref[...]
    la = _log_sigmoid(x) * (1.0 / GLA_GATE_NORMALIZER)
    tri = tri_ref[...]
    for c in range(tb // GLA_CHUNK):
        rows = slice(c * GLA_CHUNK, (c + 1) * GLA_CHUNK)
        b_c = sum(_dot(tri, part) for part in _split3(la[rows]))
        b_ref[rows, :] = b_c
        mild_ref[c] = (jnp.min(b_c[GLA_CHUNK - 1:]) > -GLA_MILD_DECAY).astype(jnp.int32)

    def chunk(c, carry):
        r0 = pl.multiple_of(c * GLA_CHUNK, GLA_CHUNK)
        rows = pl.ds(r0, GLA_CHUNK)
        b_all = b_ref[rows, :]
        mild = mild_ref[c] > 0

        def heads(chunk_head):
            for hh in range(GLA_HEADS):
                kq = slice(hh * GLA_HK, (hh + 1) * GLA_HK)
                q = p_ref[rows, kq]
                k = p_ref[rows, GLA_DK + hh * GLA_HK:GLA_DK + (hh + 1) * GLA_HK]
                v = p_ref[rows, 2 * GLA_DK + hh * GLA_HV:2 * GLA_DK + (hh + 1) * GLA_HV]
                o, s_new = chunk_head(q, k, v, b_all[:, kq], s_ref[hh])
                s_ref[hh] = s_new
                oc_ref[rows, hh * GLA_HV:(hh + 1) * GLA_HV] = o

        pl.when(mild)(functools.partial(heads, _gla_chunk_head_mild))
        pl.when(jnp.logical_not(mild))(functools.partial(heads, _gla_chunk_head))
        return carry

    lax.fori_loop(0, tb // GLA_CHUNK, chunk, 0)

    gn = gn_ref[...]
    normed = []
    for hh in range(GLA_HEADS):
        oh = oc_ref[:, hh * GLA_HV:(hh + 1) * GLA_HV]
        normed.append(oh * lax.rsqrt(jnp.mean(oh * oh, axis=-1, keepdims=True) + RMS_EPS) * gn)
    gate = p_ref[:, 2 * GLA_DK + GLA_DV:]
    y = jnp.concatenate(normed, axis=-1) * (gate * _sigmoid(gate))
    z = _dot(y.astype(BF16), wout_ref[...])
    o_ref[0] = h + _rms(z, g3_ref[...])


def _gla_mixer(h3, g_pre, g_post, w_in, w_a, w_gu, b_gate, g_norm, w_out, tri, *, tb):
    bsz, lp, _ = h3.shape
    blk = pl.BlockSpec((1, tb, D_MODEL), lambda b, j: (b, j, 0))
    n_in = 2 * GLA_DK + 2 * GLA_DV
    return pl.pallas_call(
        functools.partial(_gla_body, tb=tb),
        grid=(bsz, lp // tb),
        in_specs=[blk, _const_spec((1, D_MODEL)), _const_spec((1, D_MODEL)),
                  _const_spec((D_MODEL, n_in)), _const_spec((D_MODEL, LANES)),
                  _const_spec((LANES, GLA_DK)), _const_spec((1, GLA_DK)), _const_spec((1, GLA_HV)),
                  _const_spec((GLA_DV, D_MODEL)), _const_spec((GLA_CHUNK, GLA_CHUNK))],
        out_specs=blk,
        out_shape=jax.ShapeDtypeStruct(h3.shape, F32),
        scratch_shapes=[pltpu.VMEM((GLA_HEADS, GLA_HV, GLA_HK), F32),
                        pltpu.VMEM((tb, n_in), F32),
                        pltpu.VMEM((tb, GLA_DK), F32),
                        pltpu.VMEM((tb, GLA_DV), F32),
                        pltpu.SMEM((tb // GLA_CHUNK,), jnp.int32)],
        compiler_params=_params(2),
        name="gla_mixer",
    )(h3, g_pre, g_post, w_in, w_a, w_gu, b_gate, g_norm, w_out, tri)


def _proj_in_body(h_ref, g_ref, w_ref, o_ref):
    o_ref[...] = _dot(_rms(h_ref[...], g_ref[...]).astype(BF16), w_ref[...]).astype(o_ref.dtype)


def _proj_in(h, g, w, *, tm):
    t, n = h.shape[0], w.shape[1]
    return pl.pallas_call(
        _proj_in_body,
        grid=(t // tm,),
        in_specs=[pl.BlockSpec((tm, D_MODEL), lambda i: (i, 0)), _const_spec((1, D_MODEL)),
                  _const_spec(w.shape)],
        out_specs=pl.BlockSpec((tm, n), lambda i: (i, 0)),
        out_shape=jax.ShapeDtypeStruct((t, n), BF16),
        compiler_params=_params(1),
        name="proj_in",
    )(h, g, w)


def _proj_out_body(h_ref, x_ref, g_ref, w_ref, o_ref):
    o_ref[...] = h_ref[...] + _rms(_dot(x_ref[...], w_ref[...]), g_ref[...])


def _proj_out(h, x, g, w, *, tm):
    t, kdim = x.shape
    row = pl.BlockSpec((tm, D_MODEL), lambda i: (i, 0))
    return pl.pallas_call(
        _proj_out_body,
        grid=(t // tm,),
        in_specs=[row, pl.BlockSpec((tm, kdim), lambda i: (i, 0)), _const_spec((1, D_MODEL)),
                  _const_spec(w.shape)],
        out_specs=row,
        out_shape=jax.ShapeDtypeStruct(h.shape, F32),
        compiler_params=_params(1),
        name="proj_out",
    )(h, x, g, w)


SB_WIDE = 3
SB_DEAD = -104.0
SB_GROUP = 2


def _sb_scores(qs, kwin, mask):
    z = _dot_nt(qs, kwin)
    ls = _log_sigmoid(z)
    return ls, jnp.where(mask, ls - z, 0.0)


def _sb_suffix(l1m, u):
    hi = l1m.astype(BF16)
    lo = (l1m - hi.astype(F32)).astype(BF16)
    return _dot(jnp.concatenate([hi, lo], axis=1), u)


def _sb_body(q_ref, k_ref, v_ref, u_ref, o_ref, *, nq):
    lane = lax.broadcasted_iota(jnp.int32, (1, LANES), 1)
    low = lane < SB_HEAD_DIM
    zero = jnp.zeros((), BF16)
    row = lax.broadcasted_iota(jnp.int32, (2 * SB_BLOCK, 1), 0) % SB_BLOCK
    col = lax.broadcasted_iota(jnp.int32, (1, SB_BLOCK), 1)
    u = u_ref[...]

    def stacked_q(m):
        q = q_ref[0, pl.ds(pl.multiple_of(m * SB_BLOCK, SB_BLOCK), SB_BLOCK), :] * (SB_HEAD_DIM ** -0.5)
        return jnp.concatenate([jnp.where(low, q, zero), jnp.where(low, zero, q)], axis=0)

    def wide(m, near_start):
        qs = stacked_q(m)
        t_glob = m * SB_BLOCK + row
        kparts, vparts, masks = [], [], []
        for bi in range(SB_WIDE):
            jv = m - (SB_WIDE - 1) + bi
            k0 = pl.multiple_of(jnp.maximum(jv, 0) * SB_BLOCK, SB_BLOCK)
            kparts.append(k_ref[0, pl.ds(k0, SB_BLOCK), :])
            vparts.append(v_ref[0, pl.ds(k0, SB_BLOCK), :])
            s_glob = jv * SB_BLOCK + col
            causal = s_glob < t_glob if bi == SB_WIDE - 1 else None
            if near_start:
                real = jnp.broadcast_to(s_glob >= PAD, (2 * SB_BLOCK, SB_BLOCK))
                causal = real if causal is None else causal & real
            masks.append(causal)
        z = _dot_nt(qs, jnp.concatenate(kparts, axis=0))
        lsp, cs = [], []
        for bi in range(SB_WIDE):
            zb = z[:, bi * SB_BLOCK:(bi + 1) * SB_BLOCK]
            ls = _log_sigmoid(zb)
            l1m = ls - zb
            if masks[bi] is not None:
                l1m = jnp.where(masks[bi], l1m, 0.0)
            lsp.append(ls)
            cs.append(_sb_suffix(l1m, u))
        ws, later = [], None
        for bi in reversed(range(SB_WIDE)):
            tail = cs[bi][:, :SB_BLOCK]
            w = jnp.exp(lsp[bi] + (tail if later is None else tail + later))
            ws.append(w if masks[bi] is None else jnp.where(masks[bi], w, 0.0))
            tot = cs[bi][:, SB_BLOCK:]
            later = tot if later is None else later + tot
        w = jnp.concatenate(ws[::-1], axis=1)
        acc = _dot(w.astype(BF16), jnp.concatenate(vparts, axis=0))
        return later, acc

    def alive(c):
        return (jnp.max(c) >= SB_DEAD).astype(jnp.int32)

    def finish(m, car, acc):
        def cond(st):
            return (st[0] >= 0) & (st[1] > 0)

        def body(st):
            j, _, car, acc = st
            k0 = pl.multiple_of(j * SB_BLOCK, SB_BLOCK)
            mask = jnp.broadcast_to(k0 + col >= PAD, (2 * SB_BLOCK, SB_BLOCK))
            ls, l1m = _sb_scores(stacked_q(m), k_ref[0, pl.ds(k0, SB_BLOCK), :], mask)
            cs = _sb_suffix(l1m, u)
            w = jnp.where(mask, jnp.exp(ls + cs[:, :SB_BLOCK] + car), 0.0)
            acc = acc + _dot(w.astype(BF16), v_ref[0, pl.ds(k0, SB_BLOCK), :])
            car = car + cs[:, SB_BLOCK:]
            return j - 1, alive(car), car, acc

        _, _, _, acc = lax.while_loop(cond, body, (m - SB_WIDE, alive(car), car, acc))
        out = jnp.where(low, acc[:SB_BLOCK], acc[SB_BLOCK:])
        o_ref[0, pl.ds(pl.multiple_of(m * SB_BLOCK, SB_BLOCK), SB_BLOCK), :] = out.astype(o_ref.dtype)

    def group(ms, near_start):
        states = [wide(m, near_start) for m in ms]
        for m, (car, acc) in zip(ms, states):
            finish(m, car, acc)

    n_start = -(-SB_WIDE // SB_GROUP)
    for gi in range(n_start):
        group([jnp.int32(m) for m in range(gi * SB_GROUP, (gi + 1) * SB_GROUP) if m < nq], True)

    def steady(gi, carry):
        group([gi * SB_GROUP + r for r in range(SB_GROUP)], False)
        return carry

    lax.fori_loop(n_start, nq // SB_GROUP, steady, 0)
    group([jnp.int32(m) for m in range(max(nq // SB_GROUP, n_start) * SB_GROUP, nq)], False)


def _sb_attention(qkv3, u2):
    bsz, lp, _ = qkv3.shape
    n_pairs = D_MODEL // LANES
    assert lp >= SB_WIDE * SB_BLOCK

    def col_spec(offset):
        return pl.BlockSpec((1, lp, LANES), lambda b, p: (b, 0, offset + p))

    return pl.pallas_call(
        functools.partial(_sb_body, nq=lp // SB_BLOCK),
        grid=(bsz, n_pairs),
        in_specs=[col_spec(0), col_spec(n_pairs), col_spec(2 * n_pairs),
                  _const_spec((2 * SB_BLOCK, 2 * SB_BLOCK))],
        out_specs=pl.BlockSpec((1, lp, LANES), lambda b, p: (b, 0, p)),
        out_shape=jax.ShapeDtypeStruct((bsz, lp, D_MODEL), BF16),
        compiler_params=_params(2),
        name="sb_attention",
    )(qkv3, qkv3, qkv3, u2)


def _suffix_sum_matrix():
    j = jnp.arange(SB_BLOCK)[:, None]
    s = jnp.arange(2 * SB_BLOCK)[None, :]
    u = ((j > s) | (s >= SB_BLOCK)).astype(BF16)
    return jnp.concatenate([u, u], axis=0)


def kernel(x, meta_tokens, norm_gains, ffn_w_in, ffn_w_out, conv_w_in, conv_w, conv_w_out,
           gla_w_in, gla_w_gate_up, gla_b_gate, gla_norm, gla_w_out, sb_w_in, sb_w_out):
    bsz, seq, _ = x.shape
    depth = norm_gains.shape[0]
    lp = PAD + N_META + seq
    assert lp % SB_BLOCK == 0 and lp % GLA_CHUNK == 0
    t = bsz * lp
    tm = _row_tile(t, (512, 384, 256, 128))
    tb = _row_tile(lp, (384, 256, 128, 64))

    seq_head = jnp.concatenate([jnp.zeros((PAD, D_MODEL), x.dtype), meta_tokens.astype(x.dtype)], axis=0)
    pps = lp // SB_BLOCK
    tm_out = _row_tile(bsz * seq, (512, 256, 128))

    gains = norm_gains.reshape(depth, -1, 1, D_MODEL)
    ffn_w_in, ffn_w_out = ffn_w_in.astype(BF16), ffn_w_out.astype(BF16)
    tri = (jnp.arange(GLA_CHUNK)[:, None] >= jnp.arange(GLA_CHUNK)[None, :]).astype(BF16)
    u2 = _suffix_sum_matrix()

    h = x.reshape(bsz * seq, D_MODEL)
    for i in range(depth):
        g = gains[i]
        if i == 0:
            h = _ffn(h, g[0], g[1], ffn_w_in, ffn_w_out, i, 0, tm=tm, mode="enter", seq_head=seq_head,
                     pieces_per_seq=pps)
        else:
            h = _ffn(h, g[0], g[1], ffn_w_in, ffn_w_out, i, 0, tm=tm)
        kind, j = i % 3, i // 3
        if kind == 0:
            cw = jnp.pad(conv_w[j], ((0, 8 - CONV_WIDTH), (0, 0)))
            h = _conv_mixer(h, g[2], g[3], cw, conv_w_in[j].astype(BF16), conv_w_out[j].astype(BF16),
                            tm=tm, lp=lp)
        elif kind == 1:
            n_main = 2 * GLA_DK + 2 * GLA_DV
            w_a = jnp.pad(gla_w_in[j][:, n_main:], ((0, 0), (0, LANES - GLA_RANK))).astype(BF16)
            w_gu = jnp.pad(gla_w_gate_up[j], ((0, LANES - GLA_RANK), (0, 0))).astype(BF16)
            h = _gla_mixer(h.reshape(bsz, lp, D_MODEL), g[2], g[3], gla_w_in[j][:, :n_main].astype(BF16),
                           w_a, w_gu, gla_b_gate[j][None], gla_norm[j][None], gla_w_out[j].astype(BF16),
                           tri, tb=tb).reshape(t, D_MODEL)
        else:
            qkv = _proj_in(h, g[2], sb_w_in[j].astype(BF16), tm=tm)
            o = _sb_attention(qkv.reshape(bsz, lp, 3 * D_MODEL), u2)
            h = _proj_out(h, o.reshape(t, D_MODEL), g[3], sb_w_out[j].astype(BF16), tm=tm)
        if i == depth - 1:
            h = _ffn(h, g[4], g[5], ffn_w_in, ffn_w_out, i, 1, tm=tm_out, mode="leave", pieces_per_seq=pps)
        else:
            h = _ffn(h, g[4], g[5], ffn_w_in, ffn_w_out, i, 1, tm=tm)
    return h.reshape(bsz, seq, D_MODEL)
```

```python
import functools

import jax
import jax.numpy as jnp
from jax import lax
from jax.experimental import pallas as pl
from jax.experimental.pallas import tpu as pltpu

F32 = jnp.float32
BF16 = jnp.bfloat16

D_MODEL = 1024
D_FF = 2816
N_META = 16
RMS_EPS = 1e-6
CONV_WIDTH = 3
GLA_HEADS = 4
GLA_HK = 128
GLA_HV = 256
GLA_DK = GLA_HEADS * GLA_HK
GLA_DV = GLA_HEADS * GLA_HV
GLA_RANK = 16
GLA_GATE_NORMALIZER = 16.0
GLA_CHUNK = 64
GLA_SUB = 16
GLA_MILD_DECAY = 40.0
SB_HEAD_DIM = 64
SB_BLOCK = 128
LANES = 128
PAD = SB_BLOCK - N_META

VMEM_LIMIT_BYTES = 56 * 1024 * 1024


def _row_tile(total, candidates=(1024, 512, 384, 256, 128)):
    for c in candidates:
        if total % c == 0:
            return c
    raise ValueError(f"no row tile for {total}")


def _rms(x, g):
    return x * lax.rsqrt(jnp.mean(x * x, axis=-1, keepdims=True) + RMS_EPS) * g


def _dot(a, b):
    return jnp.dot(a, b, preferred_element_type=F32)


def _dot_nt(a, b):
    return lax.dot_general(a, b, (((1,), (1,)), ((), ())), preferred_element_type=F32)


def _dot_tn(a, b):
    return lax.dot_general(a, b, (((0,), (0,)), ((), ())), preferred_element_type=F32)


def _sigmoid(x):
    return 1.0 / (1.0 + jnp.exp(-x))


def _log_sigmoid(x):
    return jnp.minimum(x, 0.0) - jnp.log(1.0 + jnp.exp(-jnp.abs(x)))


def _const_spec(shape):
    return pl.BlockSpec(shape, lambda *_: (0,) * len(shape), pipeline_mode=pl.Buffered(1))


def _params(n_axes):
    return pltpu.CompilerParams(dimension_semantics=("arbitrary",) * n_axes,
                                vmem_limit_bytes=VMEM_LIMIT_BYTES)


def _ffn_body(*refs, n_pieces, pieces_per_seq):
    pieces, rest = refs[:n_pieces], refs[n_pieces:]
    head_ref = None
    if pieces_per_seq is not None:
        head_ref, rest = rest[0], rest[1:]
    g0_ref, g1_ref, win_ref, wout_ref, o_ref = rest
    parts = []
    for r, piece in enumerate(pieces):
        part = piece[...]
        if head_ref is not None:
            is_head = (pl.program_id(0) * n_pieces + r) % pieces_per_seq == 0
            part = jnp.where(is_head, head_ref[...], part)
        parts.append(part)
    h = parts[0] if n_pieces == 1 else jnp.concatenate(parts, axis=0)
    xb = _rms(h, g0_ref[...]).astype(BF16)
    gate = _dot(xb, win_ref[:, :D_FF])
    up = _dot(xb, win_ref[:, D_FF:])
    a = (gate * _sigmoid(gate) * up).astype(BF16)
    o_ref[...] = h + 0.5 * _rms(_dot(a, wout_ref[...]), g1_ref[...])


def _ffn(src, g_pre, g_post, w_in, w_out, layer, which, *, tm, mode="stream", seq_head=None, pieces_per_seq=None):
    weights = [pl.BlockSpec((None, None, D_MODEL, 2 * D_FF), lambda i: (layer, which, 0, 0),
                            pipeline_mode=pl.Buffered(1)),
               pl.BlockSpec((None, None, D_FF, D_MODEL), lambda i: (layer, which, 0, 0),
                            pipeline_mode=pl.Buffered(1))]
    gains = [_const_spec((1, D_MODEL)), _const_spec((1, D_MODEL))]
    if mode == "stream":
        n, out_rows = 1, src.shape[0]
        pieces, extra, extra_specs = [pl.BlockSpec((tm, D_MODEL), lambda i: (i, 0))], [], []
    else:
        n = tm // SB_BLOCK
        x_per_seq = pieces_per_seq - 1
        if mode == "enter":
            index = lambda i, r: jnp.maximum(i * n + r - (i * n + r) // pieces_per_seq - 1, 0)
            out_rows = src.shape[0] // x_per_seq * pieces_per_seq
            extra, extra_specs = [seq_head], [_const_spec((SB_BLOCK, D_MODEL))]
        else:
            index = lambda i, r: i * n + r + (i * n + r) // x_per_seq + 1
            out_rows = src.shape[0] // pieces_per_seq * x_per_seq
            extra, extra_specs = [], []
        pieces = [pl.BlockSpec((SB_BLOCK, D_MODEL), functools.partial(lambda i, r: (index(i, r), 0), r=r))
                  for r in range(n)]
    return pl.pallas_call(
        functools.partial(_ffn_body, n_pieces=n, pieces_per_seq=pieces_per_seq if mode == "enter" else None),
        grid=(out_rows // tm,),
        in_specs=pieces + extra_specs + gains + weights,
        out_specs=pl.BlockSpec((tm, D_MODEL), lambda i: (i, 0)),
        out_shape=jax.ShapeDtypeStruct((out_rows, D_MODEL), F32),
        compiler_params=_params(1),
        name="ffn",
    )(*([src] * n), *extra, g_pre, g_post, w_in, w_out)


def _conv_body(h_ref, halo_ref, g2_ref, g3_ref, cw_ref, win_ref, wout_ref, o_ref, *, tm, lp):
    i = pl.program_id(0)
    h = h_ref[...]
    g2 = g2_ref[...]
    xb = _rms(h, g2).astype(BF16)
    pc = _dot(xb, win_ref[:, D_MODEL:2 * D_MODEL])
    ph_ = _dot(xb, win_ref[:, 2 * D_MODEL:])
    row = lax.broadcasted_iota(jnp.int32, (tm, 1), 0)
    pos = (i * tm + row) % lp
    uu = jnp.where(pos >= PAD, pc * ph_, 0.0)
    xh = _rms(halo_ref[...], g2).astype(BF16)
    ph = _dot(xh, win_ref[:, D_MODEL:])
    hrow = lax.broadcasted_iota(jnp.int32, (8, 1), 0)
    hpos = (i * tm - 8 + hrow + lp) % lp
    uh = jnp.where(hpos >= PAD, ph[:, :D_MODEL] * ph[:, D_MODEL:], 0.0)
    prev1 = jnp.where(row == 0, uh[7:8], pltpu.roll(uu, 1, 0))
    prev2 = jnp.where(row == 0, uh[6:7], jnp.where(row == 1, uh[7:8], pltpu.roll(uu, 2, 0)))
    cw = cw_ref[...]
    conv = cw[2:3] * uu + cw[1:2] * prev1 + cw[0:1] * prev2
    pb = _dot(xb, win_ref[:, :D_MODEL])
    y = _dot((pb * conv).astype(BF16), wout_ref[...])
    o_ref[...] = h + _rms(y, g3_ref[...])


def _conv_mixer(h, g_pre, g_post, conv_w, w_in, w_out, *, tm, lp):
    t = h.shape[0]
    row = pl.BlockSpec((tm, D_MODEL), lambda i: (i, 0))
    halo = pl.BlockSpec((8, D_MODEL), lambda i: (jnp.maximum(i * (tm // 8) - 1, 0), 0))
    return pl.pallas_call(
        functools.partial(_conv_body, tm=tm, lp=lp),
        grid=(t // tm,),
        in_specs=[row, halo, _const_spec((1, D_MODEL)), _const_spec((1, D_MODEL)),
                  _const_spec((8, D_MODEL)), _const_spec((D_MODEL, 3 * D_MODEL)),
                  _const_spec((D_MODEL, D_MODEL))],
        out_specs=row,
        out_shape=jax.ShapeDtypeStruct(h.shape, F32),
        compiler_params=_params(1),
        name="conv_mixer",
    )(h, h, g_pre, g_post, conv_w, w_in, w_out)


def _split3(x):
    hi = x.astype(BF16)
    r = x - hi.astype(F32)
    mid = r.astype(BF16)
    lo = (r - mid.astype(F32)).astype(BF16)
    return hi, mid, lo


def _gla_chunk_head(q, k, v, b, s_t):
    n_sub = GLA_CHUNK // GLA_SUB
    sub_row = lax.broadcasted_iota(jnp.int32, (GLA_SUB, 1), 0)
    col = lax.broadcasted_iota(jnp.int32, (1, GLA_CHUNK), 1)
    att_rows = []
    for i in range(n_sub):
        r0 = i * GLA_SUB
        qd, kd, bd = q[r0:r0 + GLA_SUB], k[r0:r0 + GLA_SUB], b[r0:r0 + GLA_SUB]
        blk = jnp.zeros((GLA_SUB, GLA_CHUNK), F32)
        if i > 0:
            b_ref = b[r0 - 1:r0]
            q_t = (qd * jnp.exp(bd - b_ref)).astype(BF16)
            k_t = (k * jnp.exp(jnp.minimum(b_ref - b, 0.0))).astype(BF16)
            blk = jnp.where(col < r0, _dot_nt(q_t, k_t), 0.0)
        for s in range(GLA_SUB):
            e = jnp.exp(jnp.where(sub_row >= s, bd - bd[s:s + 1], -jnp.inf))
            a_col = jnp.sum(qd * e * kd[s:s + 1], axis=-1, keepdims=True)
            blk = jnp.where(col == r0 + s, a_col, blk)
        att_rows.append(blk)
    att = jnp.concatenate(att_rows, axis=0)
    o = _dot(att.astype(BF16), v.astype(BF16))
    o = o + _dot_nt((q * jnp.exp(b)).astype(BF16), s_t.astype(BF16))
    b_last = b[GLA_CHUNK - 1:GLA_CHUNK]
    k_dec = (k * jnp.exp(b_last - b)).astype(BF16)
    s_new = s_t * jnp.exp(b_last) + _dot_tn(v.astype(BF16), k_dec)
    return o, s_new


def _gla_chunk_mild(heads):
    t_row = lax.broadcasted_iota(jnp.int32, (GLA_CHUNK, 1), 0)
    s_col = lax.broadcasted_iota(jnp.int32, (1, GLA_CHUNK), 1)
    staged = []
    for q, k, v, b, s_t in heads:
        q_dec = (q * jnp.exp(b)).astype(BF16)
        att = _dot_nt(q_dec, (k * jnp.exp(-b)).astype(BF16))
        o_state = _dot_nt(q_dec, s_t.astype(BF16))
        b_last = b[GLA_CHUNK - 1:GLA_CHUNK]
        vb = v.astype(BF16)
        s_new = s_t * jnp.exp(b_last) + _dot_tn(vb, (k * jnp.exp(b_last - b)).astype(BF16))
        staged.append((att, o_state, vb, s_new))
    out = []
    for att, o_state, vb, s_new in staged:
        att = jnp.where(s_col <= t_row, att, 0.0).astype(BF16)
        out.append((_dot(att, vb) + o_state, s_new))
    return out


def _gla_body(h_ref, g2_ref, g3_ref, win_ref, wa_ref, wgu_ref, bg_ref, gn_ref, wout_ref, tri_ref,
              o_ref, s_ref, p_ref, b_ref, oc_ref, mild_ref, *, tb):
    jb = pl.program_id(1)

    @pl.when(jb == 0)
    def _():
        s_ref[...] = jnp.zeros_like(s_ref)

    h = h_ref[0]
    ub = _rms(h, g2_ref[...]).astype(BF16)
    p = _dot(ub, win_ref[...])
    pos = jb * tb + lax.broadcasted_iota(jnp.int32, (tb, 1), 0)
    valid = pos >= PAD
    p_ref[:, :GLA_DK] = p[:, :GLA_DK] * (GLA_HK ** -0.5)
    p_ref[:, GLA_DK:2 * GLA_DK] = jnp.where(valid, p[:, GLA_DK:2 * GLA_DK], 0.0)
    p_ref[:, 2 * GLA_DK:2 * GLA_DK + GLA_DV] = jnp.where(valid, p[:, 2 * GLA_DK:2 * GLA_DK + GLA_DV], 0.0)
    p_ref[:, 2 * GLA_DK + GLA_DV:] = p[:, 2 * GLA_DK + GLA_DV:]
    a = _dot(ub, wa_ref[...]).astype(BF16)
    x = _dot(a, wgu_ref[...]) + bg_ref[...]
    la = _log_sigmoid(x) * (1.0 / GLA_GATE_NORMALIZER)
    tri = tri_ref[...]
    for c in range(tb // GLA_CHUNK):
        rows = slice(c * GLA_CHUNK, (c + 1) * GLA_CHUNK)
        b_c = sum(_dot(tri, part) for part in _split3(la[rows]))
        b_ref[rows, :] = b_c
        mild_ref[c] = (jnp.min(b_c[GLA_CHUNK - 1:]) > -GLA_MILD_DECAY).astype(jnp.int32)

    def chunk(c, carry):
        r0 = pl.multiple_of(c * GLA_CHUNK, GLA_CHUNK)
        rows = pl.ds(r0, GLA_CHUNK)
        b_all = b_ref[rows, :]
        mild = mild_ref[c] > 0

        def run(chunk_fn):
            heads = []
            for hh in range(GLA_HEADS):
                kq = slice(hh * GLA_HK, (hh + 1) * GLA_HK)
                q = p_ref[rows, kq]
                k = p_ref[rows, GLA_DK + hh * GLA_HK:GLA_DK + (hh + 1) * GLA_HK]
                v = p_ref[rows, 2 * GLA_DK + hh * GLA_HV:2 * GLA_DK + (hh + 1) * GLA_HV]
                heads.append((q, k, v, b_all[:, kq], s_ref[hh]))
            for hh, (o, s_new) in enumerate(chunk_fn(heads)):
                s_ref[hh] = s_new
                oc_ref[rows, hh * GLA_HV:(hh + 1) * GLA_HV] = o

        pl.when(mild)(functools.partial(run, _gla_chunk_mild))
        pl.when(jnp.logical_not(mild))(
            functools.partial(run, lambda heads: [_gla_chunk_head(*head) for head in heads]))
        return carry

    lax.fori_loop(0, tb // GLA_CHUNK, chunk, 0)

    gn = gn_ref[...]
    normed = []
    for hh in range(GLA_HEADS):
        oh = oc_ref[:, hh * GLA_HV:(hh + 1) * GLA_HV]
        normed.append(oh * lax.rsqrt(jnp.mean(oh * oh, axis=-1, keepdims=True) + RMS_EPS) * gn)
    gate = p_ref[:, 2 * GLA_DK + GLA_DV:]
    y = jnp.concatenate(normed, axis=-1) * (gate * _sigmoid(gate))
    z = _dot(y.astype(BF16), wout_ref[...])
    o_ref[0] = h + _rms(z, g3_ref[...])


def _gla_mixer(h3, g_pre, g_post, w_in, w_a, w_gu, b_gate, g_norm, w_out, tri, *, tb):
    bsz, lp, _ = h3.shape
    blk = pl.BlockSpec((1, tb, D_MODEL), lambda b, j: (b, j, 0))
    n_in = 2 * GLA_DK + 2 * GLA_DV
    return pl.pallas_call(
        functools.partial(_gla_body, tb=tb),
        grid=(bsz, lp // tb),
        in_specs=[blk, _const_spec((1, D_MODEL)), _const_spec((1, D_MODEL)),
                  _const_spec((D_MODEL, n_in)), _const_spec((D_MODEL, LANES)),
                  _const_spec((LANES, GLA_DK)), _const_spec((1, GLA_DK)), _const_spec((1, GLA_HV)),
                  _const_spec((GLA_DV, D_MODEL)), _const_spec((GLA_CHUNK, GLA_CHUNK))],
        out_specs=blk,
        out_shape=jax.ShapeDtypeStruct(h3.shape, F32),
        scratch_shapes=[pltpu.VMEM((GLA_HEADS, GLA_HV, GLA_HK), F32),
                        pltpu.VMEM((tb, n_in), F32),
                        pltpu.VMEM((tb, GLA_DK), F32),
                        pltpu.VMEM((tb, GLA_DV), F32),
                        pltpu.SMEM((tb // GLA_CHUNK,), jnp.int32)],
        compiler_params=_params(2),
        name="gla_mixer",
    )(h3, g_pre, g_post, w_in, w_a, w_gu, b_gate, g_norm, w_out, tri)


def _proj_in_body(h_ref, g_ref, w_ref, o_ref):
    o_ref[...] = _dot(_rms(h_ref[...], g_ref[...]).astype(BF16), w_ref[...]).astype(o_ref.dtype)


def _proj_in(h, g, w, *, tm):
    t, n = h.shape[0], w.shape[1]
    return pl.pallas_call(
        _proj_in_body,
        grid=(t // tm,),
        in_specs=[pl.BlockSpec((tm, D_MODEL), lambda i: (i, 0)), _const_spec((1, D_MODEL)),
                  _const_spec(w.shape)],
        out_specs=pl.BlockSpec((tm, n), lambda i: (i, 0)),
        out_shape=jax.ShapeDtypeStruct((t, n), BF16),
        compiler_params=_params(1),
        name="proj_in",
    )(h, g, w)


def _proj_out_body(h_ref, x_ref, g_ref, w_ref, o_ref):
    o_ref[...] = h_ref[...] + _rms(_dot(x_ref[...], w_ref[...]), g_ref[...])


def _proj_out(h, x, g, w, *, tm):
    t, kdim = x.shape
    row = pl.BlockSpec((tm, D_MODEL), lambda i: (i, 0))
    return pl.pallas_call(
        _proj_out_body,
        grid=(t // tm,),
        in_specs=[row, pl.BlockSpec((tm, kdim), lambda i: (i, 0)), _const_spec((1, D_MODEL)),
                  _const_spec(w.shape)],
        out_specs=row,
        out_shape=jax.ShapeDtypeStruct(h.shape, F32),
        compiler_params=_params(1),
        name="proj_out",
    )(h, x, g, w)


SB_WIDE = 3
SB_DEAD = -104.0
SB_GROUP = 3


def _sb_scores(qs, kwin, mask):
    z = _dot_nt(qs, kwin)
    ls = _log_sigmoid(z)
    return ls, jnp.where(mask, ls - z, 0.0)


def _sb_suffix(l1m, u):
    hi = l1m.astype(BF16)
    lo = (l1m - hi.astype(F32)).astype(BF16)
    return _dot(jnp.concatenate([hi, lo], axis=1), u)


def _sb_body(q_ref, k_ref, v_ref, u_ref, o_ref, *, nq):
    lane = lax.broadcasted_iota(jnp.int32, (1, LANES), 1)
    low = lane < SB_HEAD_DIM
    zero = jnp.zeros((), BF16)
    row = lax.broadcasted_iota(jnp.int32, (2 * SB_BLOCK, 1), 0) % SB_BLOCK
    col = lax.broadcasted_iota(jnp.int32, (1, SB_BLOCK), 1)
    u = u_ref[...]

    def stacked_q(m):
        q = q_ref[0, pl.ds(pl.multiple_of(m * SB_BLOCK, SB_BLOCK), SB_BLOCK), :] * (SB_HEAD_DIM ** -0.5)
        return jnp.concatenate([jnp.where(low, q, zero), jnp.where(low, zero, q)], axis=0)

    def wide(m, near_start):
        qs = stacked_q(m)
        t_glob = m * SB_BLOCK + row
        kparts, vparts, masks = [], [], []
        for bi in range(SB_WIDE):
            jv = m - (SB_WIDE - 1) + bi
            k0 = pl.multiple_of(jnp.maximum(jv, 0) * SB_BLOCK, SB_BLOCK)
            kparts.append(k_ref[0, pl.ds(k0, SB_BLOCK), :])
            vparts.append(v_ref[0, pl.ds(k0, SB_BLOCK), :])
            s_glob = jv * SB_BLOCK + col
            causal = s_glob < t_glob if bi == SB_WIDE - 1 else None
            if near_start:
                real = jnp.broadcast_to(s_glob >= PAD, (2 * SB_BLOCK, SB_BLOCK))
                causal = real if causal is None else causal & real
            masks.append(causal)
        z = _dot_nt(qs, jnp.concatenate(kparts, axis=0))
        yield
        lsp, cs = [], []
        for bi in range(SB_WIDE):
            zb = z[:, bi * SB_BLOCK:(bi + 1) * SB_BLOCK]
            ls = _log_sigmoid(zb)
            l1m = ls - zb
            if masks[bi] is not None:
                l1m = jnp.where(masks[bi], l1m, 0.0)
            lsp.append(ls)
            cs.append(_sb_suffix(l1m, u))
        yield
        ws, later = [], None
        for bi in reversed(range(SB_WIDE)):
            tail = cs[bi][:, :SB_BLOCK]
            w = jnp.exp(lsp[bi] + (tail if later is None else tail + later))
            ws.append(w if masks[bi] is None else jnp.where(masks[bi], w, 0.0))
            tot = cs[bi][:, SB_BLOCK:]
            later = tot if later is None else later + tot
        w = jnp.concatenate(ws[::-1], axis=1)
        acc = _dot(w.astype(BF16), jnp.concatenate(vparts, axis=0))
        return later, acc

    def alive(c):
        return (jnp.max(c) >= SB_DEAD).astype(jnp.int32)

    def finish(m, car, acc):
        def cond(st):
            return (st[0] >= 0) & (st[1] > 0)

        def body(st):
            j, _, car, acc = st
            k0 = pl.multiple_of(j * SB_BLOCK, SB_BLOCK)
            mask = jnp.broadcast_to(k0 + col >= PAD, (2 * SB_BLOCK, SB_BLOCK))
            ls, l1m = _sb_scores(stacked_q(m), k_ref[0, pl.ds(k0, SB_BLOCK), :], mask)
            cs = _sb_suffix(l1m, u)
            w = jnp.where(mask, jnp.exp(ls + cs[:, :SB_BLOCK] + car), 0.0)
            acc = acc + _dot(w.astype(BF16), v_ref[0, pl.ds(k0, SB_BLOCK), :])
            car = car + cs[:, SB_BLOCK:]
            return j - 1, alive(car), car, acc

        _, _, _, acc = lax.while_loop(cond, body, (m - SB_WIDE, alive(car), car, acc))
        out = jnp.where(low, acc[:SB_BLOCK], acc[SB_BLOCK:])
        o_ref[0, pl.ds(pl.multiple_of(m * SB_BLOCK, SB_BLOCK), SB_BLOCK), :] = out.astype(o_ref.dtype)

    def group(ms, near_start):
        chains = [wide(m, near_start) for m in ms]
        states = []
        for _ in range(2):
            for chain in chains:
                next(chain)
        for chain in chains:
            try:
                next(chain)
            except StopIteration as done:
                states.append(done.value)
        for m, (car, acc) in zip(ms, states):
            finish(m, car, acc)

    n_start = -(-SB_WIDE // SB_GROUP)
    for gi in range(n_start):
        group([jnp.int32(m) for m in range(gi * SB_GROUP, (gi + 1) * SB_GROUP) if m < nq], True)

    def steady(gi, carry):
        group([gi * SB_GROUP + r for r in range(SB_GROUP)], False)
        return carry

    lax.fori_loop(n_start, nq // SB_GROUP, steady, 0)
    group([jnp.int32(m) for m in range(max(nq // SB_GROUP, n_start) * SB_GROUP, nq)], False)


def _sb_attention(qkv3, u2):
    bsz, lp, _ = qkv3.shape
    n_pairs = D_MODEL // LANES
    assert lp >= SB_WIDE * SB_BLOCK

    def col_spec(offset):
        return pl.BlockSpec((1, lp, LANES), lambda b, p: (b, 0, offset + p))

    return pl.pallas_call(
        functools.partial(_sb_body, nq=lp // SB_BLOCK),
        grid=(bsz, n_pairs),
        in_specs=[col_spec(0), col_spec(n_pairs), col_spec(2 * n_pairs),
                  _const_spec((2 * SB_BLOCK, 2 * SB_BLOCK))],
        out_specs=pl.BlockSpec((1, lp, LANES), lambda b, p: (b, 0, p)),
        out_shape=jax.ShapeDtypeStruct((bsz, lp, D_MODEL), BF16),
        compiler_params=_params(2),
        name="sb_attention",
    )(qkv3, qkv3, qkv3, u2)


def _suffix_sum_matrix():
    j = jnp.arange(SB_BLOCK)[:, None]
    s = jnp.arange(2 * SB_BLOCK)[None, :]
    u = ((j > s) | (s >= SB_BLOCK)).astype(BF16)
    return jnp.concatenate([u, u], axis=0)


def kernel(x, meta_tokens, norm_gains, ffn_w_in, ffn_w_out, conv_w_in, conv_w, conv_w_out,
           gla_w_in, gla_w_gate_up, gla_b_gate, gla_norm, gla_w_out, sb_w_in, sb_w_out):
    bsz, seq, _ = x.shape
    depth = norm_gains.shape[0]
    lp = PAD + N_META + seq
    assert lp % SB_BLOCK == 0 and lp % GLA_CHUNK == 0
    t = bsz * lp
    tm = _row_tile(t, (512, 384, 256, 128))
    tb = _row_tile(lp, (384, 256, 128, 64))

    seq_head = jnp.concatenate([jnp.zeros((PAD, D_MODEL), x.dtype), meta_tokens.astype(x.dtype)], axis=0)
    pps = lp // SB_BLOCK
    tm_out = _row_tile(bsz * seq, (512, 256, 128))

    gains = norm_gains.reshape(depth, -1, 1, D_MODEL)
    ffn_w_in, ffn_w_out = ffn_w_in.astype(BF16), ffn_w_out.astype(BF16)
    tri = (jnp.arange(GLA_CHUNK)[:, None] >= jnp.arange(GLA_CHUNK)[None, :]).astype(BF16)
    u2 = _suffix_sum_matrix()

    h = x.reshape(bsz * seq, D_MODEL)
    for i in range(depth):
        g = gains[i]
        if i == 0:
            h = _ffn(h, g[0], g[1], ffn_w_in, ffn_w_out, i, 0, tm=tm, mode="enter", seq_head=seq_head,
                     pieces_per_seq=pps)
        else:
            h = _ffn(h, g[0], g[1], ffn_w_in, ffn_w_out, i, 0, tm=tm)
        kind, j = i % 3, i // 3
        if kind == 0:
            cw = jnp.pad(conv_w[j], ((0, 8 - CONV_WIDTH), (0, 0)))
            h = _conv_mixer(h, g[2], g[3], cw, conv_w_in[j].astype(BF16), conv_w_out[j].astype(BF16),
                            tm=tm, lp=lp)
        elif kind == 1:
            n_main = 2 * GLA_DK + 2 * GLA_DV
            w_a = jnp.pad(gla_w_in[j][:, n_main:], ((0, 0), (0, LANES - GLA_RANK))).astype(BF16)
            w_gu = jnp.pad(gla_w_gate_up[j], ((0, LANES - GLA_RANK), (0, 0))).astype(BF16)
            h = _gla_mixer(h.reshape(bsz, lp, D_MODEL), g[2], g[3], gla_w_in[j][:, :n_main].astype(BF16),
                           w_a, w_gu, gla_b_gate[j][None], gla_norm[j][None], gla_w_out[j].astype(BF16),
                           tri, tb=tb).reshape(t, D_MODEL)
        else:
            qkv = _proj_in(h, g[2], sb_w_in[j].astype(BF16), tm=tm)
            o = _sb_attention(qkv.reshape(bsz, lp, 3 * D_MODEL), u2)
            h = _proj_out(h, o.reshape(t, D_MODEL), g[3], sb_w_out[j].astype(BF16), tm=tm)
        if i == depth - 1:
            h = _ffn(h, g[4], g[5], ffn_w_in, ffn_w_out, i, 1, tm=tm_out, mode="leave", pieces_per_seq=pps)
        else:
            h = _ffn(h, g[4], g[5], ffn_w_in, ffn_w_out, i, 1, tm=tm)
    return h.reshape(bsz, seq, D_MODEL)
```

```python
import functools

import jax
import jax.numpy as jnp
from jax import lax
from jax.experimental import pallas as pl
from jax.experimental.pallas import tpu as pltpu

F32 = jnp.float32
BF16 = jnp.bfloat16

D_MODEL = 1024
D_FF = 2816
N_META = 16
RMS_EPS = 1e-6
CONV_WIDTH = 3
GLA_HEADS = 4
GLA_HK = 128
GLA_HV = 256
GLA_DK = GLA_HEADS * GLA_HK
GLA_DV = GLA_HEADS * GLA_HV
GLA_RANK = 16
GLA_GATE_NORMALIZER = 16.0
GLA_CHUNK = 64
GLA_SUB = 16
GLA_MILD_DECAY = 40.0
SB_HEAD_DIM = 64
SB_BLOCK = 128
LANES = 128
PAD = SB_BLOCK - N_META

VMEM_LIMIT_BYTES = 56 * 1024 * 1024


def _row_tile(total, candidates=(1024, 512, 384, 256, 128)):
    for c in candidates:
        if total % c == 0:
            return c
    raise ValueError(f"no row tile for {total}")


def _rms(x, g):
    return x * lax.rsqrt(jnp.mean(x * x, axis=-1, keepdims=True) + RMS_EPS) * g


def _dot(a, b):
    return jnp.dot(a, b, preferred_element_type=F32)


def _dot_nt(a, b):
    return lax.dot_general(a, b, (((1,), (1,)), ((), ())), preferred_element_type=F32)


def _dot_tn(a, b):
    return lax.dot_general(a, b, (((0,), (0,)), ((), ())), preferred_element_type=F32)


def _sigmoid(x):
    return 1.0 / (1.0 + jnp.exp(-x))


def _log_sigmoid(x):
    return jnp.minimum(x, 0.0) - jnp.log(1.0 + jnp.exp(-jnp.abs(x)))


def _run_interleaved(chains):
    results = [None] * len(chains)
    live = list(range(len(chains)))
    while live:
        for idx in list(live):
            try:
                next(chains[idx])
            except StopIteration as done:
                results[idx] = done.value
                live.remove(idx)
    return results


def _const_spec(shape):
    return pl.BlockSpec(shape, lambda *_: (0,) * len(shape), pipeline_mode=pl.Buffered(1))


def _params(n_axes):
    return pltpu.CompilerParams(dimension_semantics=("arbitrary",) * n_axes,
                                vmem_limit_bytes=VMEM_LIMIT_BYTES)


def _ffn_body(*refs, n_pieces, pieces_per_seq, with_mix):
    pieces, rest = refs[:n_pieces], refs[n_pieces:]
    head_ref = None
    if pieces_per_seq is not None:
        head_ref, rest = rest[0], rest[1:]
    mix_refs = None
    if with_mix:
        mix_refs, rest = rest[:3], rest[3:]
    g0_ref, g1_ref, win_ref, wout_ref, o_ref = rest
    tm = o_ref.shape[0]
    piece_rows = tm // n_pieces

    def rows_of(lo, hi):
        if n_pieces == 1:
            return pieces[0][lo:hi, :]
        parts = []
        for r in range(lo // piece_rows, hi // piece_rows):
            part = pieces[r][...]
            if head_ref is not None:
                is_head = (pl.program_id(0) * n_pieces + r) % pieces_per_seq == 0
                part = jnp.where(is_head, head_ref[...], part)
            parts.append(part)
        return parts[0] if len(parts) == 1 else jnp.concatenate(parts, axis=0)

    def chain(lo, hi):
        h = rows_of(lo, hi)
        if mix_refs is not None:
            m_ref, gm_ref, wm_ref = mix_refs
            h = h + _rms(_dot(m_ref[lo:hi, :], wm_ref[...]), gm_ref[...])
        xb = _rms(h, g0_ref[...]).astype(BF16)
        yield
        gate = _dot(xb, win_ref[:, :D_FF])
        up = _dot(xb, win_ref[:, D_FF:])
        a = (gate * _sigmoid(gate) * up).astype(BF16)
        yield
        y = _dot(a, wout_ref[...])
        yield
        o_ref[lo:hi, :] = h + 0.5 * _rms(y, g1_ref[...])

    align = 16 if n_pieces == 1 else piece_rows
    n_split = next(c for c in (3, 2, 1) if tm % (c * align) == 0)
    step = tm // n_split
    _run_interleaved([chain(c * step, (c + 1) * step) for c in range(n_split)])


def _ffn(src, g_pre, g_post, w_in, w_out, layer, which, *, tm, mode="stream", seq_head=None, pieces_per_seq=None,
         mix=None):
    weights = [pl.BlockSpec((None, None, D_MODEL, 2 * D_FF), lambda i: (layer, which, 0, 0),
                            pipeline_mode=pl.Buffered(1)),
               pl.BlockSpec((None, None, D_FF, D_MODEL), lambda i: (layer, which, 0, 0),
                            pipeline_mode=pl.Buffered(1))]
    gains = [_const_spec((1, D_MODEL)), _const_spec((1, D_MODEL))]
    if mode == "stream":
        n, out_rows = 1, src.shape[0]
        pieces, extra, extra_specs = [pl.BlockSpec((tm, D_MODEL), lambda i: (i, 0))], [], []
        if mix is not None:
            extra = list(mix)
            extra_specs = [pl.BlockSpec((tm, mix[0].shape[1]), lambda i: (i, 0)), _const_spec((1, D_MODEL)),
                           _const_spec(mix[2].shape)]
    else:
        n = tm // SB_BLOCK
        x_per_seq = pieces_per_seq - 1
        if mode == "enter":
            index = lambda i, r: jnp.maximum(i * n + r - (i * n + r) // pieces_per_seq - 1, 0)
            out_rows = src.shape[0] // x_per_seq * pieces_per_seq
            extra, extra_specs = [seq_head], [_const_spec((SB_BLOCK, D_MODEL))]
        else:
            index = lambda i, r: i * n + r + (i * n + r) // x_per_seq + 1
            out_rows = src.shape[0] // pieces_per_seq * x_per_seq
            extra, extra_specs = [], []
        pieces = [pl.BlockSpec((SB_BLOCK, D_MODEL), functools.partial(lambda i, r: (index(i, r), 0), r=r))
                  for r in range(n)]
    return pl.pallas_call(
        functools.partial(_ffn_body, n_pieces=n, pieces_per_seq=pieces_per_seq if mode == "enter" else None,
                          with_mix=mix is not None),
        grid=(out_rows // tm,),
        in_specs=pieces + extra_specs + gains + weights,
        out_specs=pl.BlockSpec((tm, D_MODEL), lambda i: (i, 0)),
        out_shape=jax.ShapeDtypeStruct((out_rows, D_MODEL), F32),
        compiler_params=_params(1),
        name="ffn",
    )(*([src] * n), *extra, g_pre, g_post, w_in, w_out)


def _conv_body(h_ref, halo_ref, g2_ref, g3_ref, cw_ref, win_ref, wout_ref, o_ref, *, tm, lp):
    i = pl.program_id(0)
    h = h_ref[...]
    g2 = g2_ref[...]
    xb = _rms(h, g2).astype(BF16)
    pc = _dot(xb, win_ref[:, D_MODEL:2 * D_MODEL])
    ph_ = _dot(xb, win_ref[:, 2 * D_MODEL:])
    row = lax.broadcasted_iota(jnp.int32, (tm, 1), 0)
    pos = (i * tm + row) % lp
    uu = jnp.where(pos >= PAD, pc * ph_, 0.0)
    xh = _rms(halo_ref[...], g2).astype(BF16)
    ph = _dot(xh, win_ref[:, D_MODEL:])
    hrow = lax.broadcasted_iota(jnp.int32, (8, 1), 0)
    hpos = (i * tm - 8 + hrow + lp) % lp
    uh = jnp.where(hpos >= PAD, ph[:, :D_MODEL] * ph[:, D_MODEL:], 0.0)
    prev1 = jnp.where(row == 0, uh[7:8], pltpu.roll(uu, 1, 0))
    prev2 = jnp.where(row == 0, uh[6:7], jnp.where(row == 1, uh[7:8], pltpu.roll(uu, 2, 0)))
    cw = cw_ref[...]
    conv = cw[2:3] * uu + cw[1:2] * prev1 + cw[0:1] * prev2
    pb = _dot(xb, win_ref[:, :D_MODEL])
    y = _dot((pb * conv).astype(BF16), wout_ref[...])
    o_ref[...] = h + _rms(y, g3_ref[...])


def _conv_mixer(h, g_pre, g_post, conv_w, w_in, w_out, *, tm, lp):
    t = h.shape[0]
    row = pl.BlockSpec((tm, D_MODEL), lambda i: (i, 0))
    halo = pl.BlockSpec((8, D_MODEL), lambda i: (jnp.maximum(i * (tm // 8) - 1, 0), 0))
    return pl.pallas_call(
        functools.partial(_conv_body, tm=tm, lp=lp),
        grid=(t // tm,),
        in_specs=[row, halo, _const_spec((1, D_MODEL)), _const_spec((1, D_MODEL)),
                  _const_spec((8, D_MODEL)), _const_spec((D_MODEL, 3 * D_MODEL)),
                  _const_spec((D_MODEL, D_MODEL))],
        out_specs=row,
        out_shape=jax.ShapeDtypeStruct(h.shape, F32),
        compiler_params=_params(1),
        name="conv_mixer",
    )(h, h, g_pre, g_post, conv_w, w_in, w_out)


def _split3(x):
    hi = x.astype(BF16)
    r = x - hi.astype(F32)
    mid = r.astype(BF16)
    lo = (r - mid.astype(F32)).astype(BF16)
    return hi, mid, lo


def _gla_chunk_head(q, k, v, b, s_t):
    n_sub = GLA_CHUNK // GLA_SUB
    sub_row = lax.broadcasted_iota(jnp.int32, (GLA_SUB, 1), 0)
    col = lax.broadcasted_iota(jnp.int32, (1, GLA_CHUNK), 1)
    att_rows = []
    for i in range(n_sub):
        r0 = i * GLA_SUB
        qd, kd, bd = q[r0:r0 + GLA_SUB], k[r0:r0 + GLA_SUB], b[r0:r0 + GLA_SUB]
        blk = jnp.zeros((GLA_SUB, GLA_CHUNK), F32)
        if i > 0:
            b_ref = b[r0 - 1:r0]
            q_t = (qd * jnp.exp(bd - b_ref)).astype(BF16)
            k_t = (k * jnp.exp(jnp.minimum(b_ref - b, 0.0))).astype(BF16)
            blk = jnp.where(col < r0, _dot_nt(q_t, k_t), 0.0)
        for s in range(GLA_SUB):
            e = jnp.exp(jnp.where(sub_row >= s, bd - bd[s:s + 1], -jnp.inf))
            a_col = jnp.sum(qd * e * kd[s:s + 1], axis=-1, keepdims=True)
            blk = jnp.where(col == r0 + s, a_col, blk)
        att_rows.append(blk)
    att = jnp.concatenate(att_rows, axis=0)
    o = _dot(att.astype(BF16), v.astype(BF16))
    o = o + _dot_nt((q * jnp.exp(b)).astype(BF16), s_t.astype(BF16))
    b_last = b[GLA_CHUNK - 1:GLA_CHUNK]
    k_dec = (k * jnp.exp(b_last - b)).astype(BF16)
    s_new = s_t * jnp.exp(b_last) + _dot_tn(v.astype(BF16), k_dec)
    return o, s_new


def _gla_chunk_mild(heads):
    t_row = lax.broadcasted_iota(jnp.int32, (GLA_CHUNK, 1), 0)
    s_col = lax.broadcasted_iota(jnp.int32, (1, GLA_CHUNK), 1)
    staged = []
    for q, k, v, b, s_t in heads:
        q_dec = (q * jnp.exp(b)).astype(BF16)
        att = _dot_nt(q_dec, (k * jnp.exp(-b)).astype(BF16))
        o_state = _dot_nt(q_dec, s_t.astype(BF16))
        b_last = b[GLA_CHUNK - 1:GLA_CHUNK]
        vb = v.astype(BF16)
        s_new = s_t * jnp.exp(b_last) + _dot_tn(vb, (k * jnp.exp(b_last - b)).astype(BF16))
        staged.append((att, o_state, vb, s_new))
    out = []
    for att, o_state, vb, s_new in staged:
        att = jnp.where(s_col <= t_row, att, 0.0).astype(BF16)
        out.append((_dot(att, vb) + o_state, s_new))
    return out


def _gla_body(h_ref, g2_ref, g3_ref, win_ref, wa_ref, wgu_ref, bg_ref, gn_ref, wout_ref, tri_ref,
              o_ref, s_ref, p_ref, b_ref, oc_ref, mild_ref, *, tb):
    jb = pl.program_id(1)

    @pl.when(jb == 0)
    def _():
        s_ref[...] = jnp.zeros_like(s_ref)

    h = h_ref[0]
    ub = _rms(h, g2_ref[...]).astype(BF16)
    p = _dot(ub, win_ref[...])
    pos = jb * tb + lax.broadcasted_iota(jnp.int32, (tb, 1), 0)
    valid = pos >= PAD
    p_ref[:, :GLA_DK] = p[:, :GLA_DK] * (GLA_HK ** -0.5)
    p_ref[:, GLA_DK:2 * GLA_DK] = jnp.where(valid, p[:, GLA_DK:2 * GLA_DK], 0.0)
    p_ref[:, 2 * GLA_DK:2 * GLA_DK + GLA_DV] = jnp.where(valid, p[:, 2 * GLA_DK:2 * GLA_DK + GLA_DV], 0.0)
    p_ref[:, 2 * GLA_DK + GLA_DV:] = p[:, 2 * GLA_DK + GLA_DV:]
    a = _dot(ub, wa_ref[...]).astype(BF16)
    x = _dot(a, wgu_ref[...]) + bg_ref[...]
    la = _log_sigmoid(x) * (1.0 / GLA_GATE_NORMALIZER)
    tri = tri_ref[...]
    for c in range(tb // GLA_CHUNK):
        rows = slice(c * GLA_CHUNK, (c + 1) * GLA_CHUNK)
        b_c = sum(_dot(tri, part) for part in _split3(la[rows]))
        b_ref[rows, :] = b_c
        mild_ref[c] = (jnp.min(b_c[GLA_CHUNK - 1:]) > -GLA_MILD_DECAY).astype(jnp.int32)

    def chunk(c, carry):
        r0 = pl.multiple_of(c * GLA_CHUNK, GLA_CHUNK)
        rows = pl.ds(r0, GLA_CHUNK)
        b_all = b_ref[rows, :]
        mild = mild_ref[c] > 0

        def run(chunk_fn):
            heads = []
            for hh in range(GLA_HEADS):
                kq = slice(hh * GLA_HK, (hh + 1) * GLA_HK)
                q = p_ref[rows, kq]
                k = p_ref[rows, GLA_DK + hh * GLA_HK:GLA_DK + (hh + 1) * GLA_HK]
                v = p_ref[rows, 2 * GLA_DK + hh * GLA_HV:2 * GLA_DK + (hh + 1) * GLA_HV]
                heads.append((q, k, v, b_all[:, kq], s_ref[hh]))
            for hh, (o, s_new) in enumerate(chunk_fn(heads)):
                s_ref[hh] = s_new
                oc_ref[rows, hh * GLA_HV:(hh + 1) * GLA_HV] = o

        pl.when(mild)(functools.partial(run, _gla_chunk_mild))
        pl.when(jnp.logical_not(mild))(
            functools.partial(run, lambda heads: [_gla_chunk_head(*head) for head in heads]))
        return carry

    lax.fori_loop(0, tb // GLA_CHUNK, chunk, 0)

    gn = gn_ref[...]
    normed = []
    for hh in range(GLA_HEADS):
        oh = oc_ref[:, hh * GLA_HV:(hh + 1) * GLA_HV]
        normed.append(oh * lax.rsqrt(jnp.mean(oh * oh, axis=-1, keepdims=True) + RMS_EPS) * gn)
    gate = p_ref[:, 2 * GLA_DK + GLA_DV:]
    y = jnp.concatenate(normed, axis=-1) * (gate * _sigmoid(gate))
    z = _dot(y.astype(BF16), wout_ref[...])
    o_ref[0] = h + _rms(z, g3_ref[...])


def _gla_mixer(h3, g_pre, g_post, w_in, w_a, w_gu, b_gate, g_norm, w_out, tri, *, tb):
    bsz, lp, _ = h3.shape
    blk = pl.BlockSpec((1, tb, D_MODEL), lambda b, j: (b, j, 0))
    n_in = 2 * GLA_DK + 2 * GLA_DV
    return pl.pallas_call(
        functools.partial(_gla_body, tb=tb),
        grid=(bsz, lp // tb),
        in_specs=[blk, _const_spec((1, D_MODEL)), _const_spec((1, D_MODEL)),
                  _const_spec((D_MODEL, n_in)), _const_spec((D_MODEL, LANES)),
                  _const_spec((LANES, GLA_DK)), _const_spec((1, GLA_DK)), _const_spec((1, GLA_HV)),
                  _const_spec((GLA_DV, D_MODEL)), _const_spec((GLA_CHUNK, GLA_CHUNK))],
        out_specs=blk,
        out_shape=jax.ShapeDtypeStruct(h3.shape, F32),
        scratch_shapes=[pltpu.VMEM((GLA_HEADS, GLA_HV, GLA_HK), F32),
                        pltpu.VMEM((tb, n_in), F32),
                        pltpu.VMEM((tb, GLA_DK), F32),
                        pltpu.VMEM((tb, GLA_DV), F32),
                        pltpu.SMEM((tb // GLA_CHUNK,), jnp.int32)],
        compiler_params=_params(2),
        name="gla_mixer",
    )(h3, g_pre, g_post, w_in, w_a, w_gu, b_gate, g_norm, w_out, tri)


def _proj_in_body(h_ref, g_ref, w_ref, o_ref):
    o_ref[...] = _dot(_rms(h_ref[...], g_ref[...]).astype(BF16), w_ref[...]).astype(o_ref.dtype)


def _proj_in(h, g, w, *, tm):
    t, n = h.shape[0], w.shape[1]
    return pl.pallas_call(
        _proj_in_body,
        grid=(t // tm,),
        in_specs=[pl.BlockSpec((tm, D_MODEL), lambda i: (i, 0)), _const_spec((1, D_MODEL)),
                  _const_spec(w.shape)],
        out_specs=pl.BlockSpec((tm, n), lambda i: (i, 0)),
        out_shape=jax.ShapeDtypeStruct((t, n), BF16),
        compiler_params=_params(1),
        name="proj_in",
    )(h, g, w)


def _proj_out_body(h_ref, x_ref, g_ref, w_ref, o_ref):
    o_ref[...] = h_ref[...] + _rms(_dot(x_ref[...], w_ref[...]), g_ref[...])


def _proj_out(h, x, g, w, *, tm):
    t, kdim = x.shape
    row = pl.BlockSpec((tm, D_MODEL), lambda i: (i, 0))
    return pl.pallas_call(
        _proj_out_body,
        grid=(t // tm,),
        in_specs=[row, pl.BlockSpec((tm, kdim), lambda i: (i, 0)), _const_spec((1, D_MODEL)),
                  _const_spec(w.shape)],
        out_specs=row,
        out_shape=jax.ShapeDtypeStruct(h.shape, F32),
        compiler_params=_params(1),
        name="proj_out",
    )(h, x, g, w)


SB_WIDE = 3
SB_DEAD = -104.0
SB_GROUP = 3


def _sb_scores(qs, kwin, mask):
    z = _dot_nt(qs, kwin)
    ls = _log_sigmoid(z)
    return ls, jnp.where(mask, ls - z, 0.0)


def _sb_suffix(l1m, u):
    hi = l1m.astype(BF16)
    lo = (l1m - hi.astype(F32)).astype(BF16)
    return _dot(jnp.concatenate([hi, lo], axis=1), u)


def _sb_body(q_ref, k_ref, v_ref, u_ref, o_ref, *, nq):
    lane = lax.broadcasted_iota(jnp.int32, (1, LANES), 1)
    low = lane < SB_HEAD_DIM
    zero = jnp.zeros((), BF16)
    row = lax.broadcasted_iota(jnp.int32, (2 * SB_BLOCK, 1), 0) % SB_BLOCK
    col = lax.broadcasted_iota(jnp.int32, (1, SB_BLOCK), 1)
    u = u_ref[...]

    def stacked_q(m):
        q = q_ref[0, pl.ds(pl.multiple_of(m * SB_BLOCK, SB_BLOCK), SB_BLOCK), :] * (SB_HEAD_DIM ** -0.5)
        return jnp.concatenate([jnp.where(low, q, zero), jnp.where(low, zero, q)], axis=0)

    def wide(m, near_start):
        qs = stacked_q(m)
        t_glob = m * SB_BLOCK + row
        kparts, vparts, masks = [], [], []
        for bi in range(SB_WIDE):
            jv = m - (SB_WIDE - 1) + bi
            k0 = pl.multiple_of(jnp.maximum(jv, 0) * SB_BLOCK, SB_BLOCK)
            kparts.append(k_ref[0, pl.ds(k0, SB_BLOCK), :])
            vparts.append(v_ref[0, pl.ds(k0, SB_BLOCK), :])
            s_glob = jv * SB_BLOCK + col
            causal = s_glob < t_glob if bi == SB_WIDE - 1 else None
            if near_start:
                real = jnp.broadcast_to(s_glob >= PAD, (2 * SB_BLOCK, SB_BLOCK))
                causal = real if causal is None else causal & real
            masks.append(causal)
        z = _dot_nt(qs, jnp.concatenate(kparts, axis=0))
        yield
        lsp, cs = [], []
        for bi in range(SB_WIDE):
            zb = z[:, bi * SB_BLOCK:(bi + 1) * SB_BLOCK]
            ls = _log_sigmoid(zb)
            l1m = ls - zb
            if masks[bi] is not None:
                l1m = jnp.where(masks[bi], l1m, 0.0)
            lsp.append(ls)
            cs.append(_sb_suffix(l1m, u))
        yield
        ws, later = [], None
        for bi in reversed(range(SB_WIDE)):
            tail = cs[bi][:, :SB_BLOCK]
            w = jnp.exp(lsp[bi] + (tail if later is None else tail + later))
            ws.append(w if masks[bi] is None else jnp.where(masks[bi], w, 0.0))
            tot = cs[bi][:, SB_BLOCK:]
            later = tot if later is None else later + tot
        w = jnp.concatenate(ws[::-1], axis=1)
        acc = _dot(w.astype(BF16), jnp.concatenate(vparts, axis=0))
        return later, acc

    def alive(c):
        return (jnp.max(c) >= SB_DEAD).astype(jnp.int32)

    def finish(m, car, acc):
        def cond(st):
            return (st[0] >= 0) & (st[1] > 0)

        def body(st):
            j, _, car, acc = st
            k0 = pl.multiple_of(j * SB_BLOCK, SB_BLOCK)
            mask = jnp.broadcast_to(k0 + col >= PAD, (2 * SB_BLOCK, SB_BLOCK))
            ls, l1m = _sb_scores(stacked_q(m), k_ref[0, pl.ds(k0, SB_BLOCK), :], mask)
            cs = _sb_suffix(l1m, u)
            w = jnp.where(mask, jnp.exp(ls + cs[:, :SB_BLOCK] + car), 0.0)
            acc = acc + _dot(w.astype(BF16), v_ref[0, pl.ds(k0, SB_BLOCK), :])
            car = car + cs[:, SB_BLOCK:]
            return j - 1, alive(car), car, acc

        _, _, _, acc = lax.while_loop(cond, body, (m - SB_WIDE, alive(car), car, acc))
        out = jnp.where(low, acc[:SB_BLOCK], acc[SB_BLOCK:])
        o_ref[0, pl.ds(pl.multiple_of(m * SB_BLOCK, SB_BLOCK), SB_BLOCK), :] = out.astype(o_ref.dtype)

    def group(ms, near_start):
        states = _run_interleaved([wide(m, near_start) for m in ms])
        for m, (car, acc) in zip(ms, states):
            finish(m, car, acc)

    n_start = -(-SB_WIDE // SB_GROUP)
    for gi in range(n_start):
        group([jnp.int32(m) for m in range(gi * SB_GROUP, (gi + 1) * SB_GROUP) if m < nq], True)

    def steady(gi, carry):
        group([gi * SB_GROUP + r for r in range(SB_GROUP)], False)
        return carry

    lax.fori_loop(n_start, nq // SB_GROUP, steady, 0)
    group([jnp.int32(m) for m in range(max(nq // SB_GROUP, n_start) * SB_GROUP, nq)], False)


def _sb_attention(qkv3, u2):
    bsz, lp, _ = qkv3.shape
    n_pairs = D_MODEL // LANES
    assert lp >= SB_WIDE * SB_BLOCK

    def col_spec(offset):
        return pl.BlockSpec((1, lp, LANES), lambda b, p: (b, 0, offset + p))

    return pl.pallas_call(
        functools.partial(_sb_body, nq=lp // SB_BLOCK),
        grid=(bsz, n_pairs),
        in_specs=[col_spec(0), col_spec(n_pairs), col_spec(2 * n_pairs),
                  _const_spec((2 * SB_BLOCK, 2 * SB_BLOCK))],
        out_specs=pl.BlockSpec((1, lp, LANES), lambda b, p: (b, 0, p)),
        out_shape=jax.ShapeDtypeStruct((bsz, lp, D_MODEL), BF16),
        compiler_params=_params(2),
        name="sb_attention",
    )(qkv3, qkv3, qkv3, u2)


def _suffix_sum_matrix():
    j = jnp.arange(SB_BLOCK)[:, None]
    s = jnp.arange(2 * SB_BLOCK)[None, :]
    u = ((j > s) | (s >= SB_BLOCK)).astype(BF16)
    return jnp.concatenate([u, u], axis=0)


def kernel(x, meta_tokens, norm_gains, ffn_w_in, ffn_w_out, conv_w_in, conv_w, conv_w_out,
           gla_w_in, gla_w_gate_up, gla_b_gate, gla_norm, gla_w_out, sb_w_in, sb_w_out):
    bsz, seq, _ = x.shape
    depth = norm_gains.shape[0]
    lp = PAD + N_META + seq
    assert lp % SB_BLOCK == 0 and lp % GLA_CHUNK == 0
    t = bsz * lp
    tm = _row_tile(t, (512, 384, 256, 128))
    tm_ffn = _row_tile(t, (768, 512, 384, 256, 128))
    tb = _row_tile(lp, (384, 256, 128, 64))

    seq_head = jnp.concatenate([jnp.zeros((PAD, D_MODEL), x.dtype), meta_tokens.astype(x.dtype)], axis=0)
    pps = lp // SB_BLOCK
    tm_out = _row_tile(bsz * seq, (512, 256, 128))

    gains = norm_gains.reshape(depth, -1, 1, D_MODEL)
    ffn_w_in, ffn_w_out = ffn_w_in.astype(BF16), ffn_w_out.astype(BF16)
    tri = (jnp.arange(GLA_CHUNK)[:, None] >= jnp.arange(GLA_CHUNK)[None, :]).astype(BF16)
    u2 = _suffix_sum_matrix()

    h = x.reshape(bsz * seq, D_MODEL)
    for i in range(depth):
        g = gains[i]
        if i == 0:
            h = _ffn(h, g[0], g[1], ffn_w_in, ffn_w_out, i, 0, tm=tm_ffn, mode="enter", seq_head=seq_head,
                     pieces_per_seq=pps)
        else:
            h = _ffn(h, g[0], g[1], ffn_w_in, ffn_w_out, i, 0, tm=tm_ffn)
        kind, j = i % 3, i // 3
        if kind == 0:
            cw = jnp.pad(conv_w[j], ((0, 8 - CONV_WIDTH), (0, 0)))
            h = _conv_mixer(h, g[2], g[3], cw, conv_w_in[j].astype(BF16), conv_w_out[j].astype(BF16),
                            tm=tm, lp=lp)
        elif kind == 1:
            n_main = 2 * GLA_DK + 2 * GLA_DV
            w_a = jnp.pad(gla_w_in[j][:, n_main:], ((0, 0), (0, LANES - GLA_RANK))).astype(BF16)
            w_gu = jnp.pad(gla_w_gate_up[j], ((0, LANES - GLA_RANK), (0, 0))).astype(BF16)
            h = _gla_mixer(h.reshape(bsz, lp, D_MODEL), g[2], g[3], gla_w_in[j][:, :n_main].astype(BF16),
                           w_a, w_gu, gla_b_gate[j][None], gla_norm[j][None], gla_w_out[j].astype(BF16),
                           tri, tb=tb).reshape(t, D_MODEL)
        else:
            qkv = _proj_in(h, g[2], sb_w_in[j].astype(BF16), tm=tm)
            o = _sb_attention(qkv.reshape(bsz, lp, 3 * D_MODEL), u2)
            mix = (o.reshape(t, D_MODEL), g[3], sb_w_out[j].astype(BF16))
            if i == depth - 1:
                h = _proj_out(h, *mix, tm=tm)
        if i == depth - 1:
            h = _ffn(h, g[4], g[5], ffn_w_in, ffn_w_out, i, 1, tm=tm_out, mode="leave", pieces_per_seq=pps)
        else:
            h = _ffn(h, g[4], g[5], ffn_w_in, ffn_w_out, i, 1, tm=tm_ffn, mix=mix if kind == 2 else None)
    return h.reshape(bsz, seq, D_MODEL)
```

```python
import functools

import jax
import jax.numpy as jnp
from jax import lax
from jax.experimental import pallas as pl
from jax.experimental.pallas import tpu as pltpu

F32 = jnp.float32
BF16 = jnp.bfloat16

D_MODEL = 1024
D_FF = 2816
N_META = 16
RMS_EPS = 1e-6
CONV_WIDTH = 3
GLA_HEADS = 4
GLA_HK = 128
GLA_HV = 256
GLA_DK = GLA_HEADS * GLA_HK
GLA_DV = GLA_HEADS * GLA_HV
GLA_RANK = 16
GLA_GATE_NORMALIZER = 16.0
GLA_CHUNK = 64
GLA_SUB = 16
GLA_MILD_DECAY = 40.0
SB_HEAD_DIM = 64
SB_BLOCK = 128
LANES = 128
PAD = SB_BLOCK - N_META

VMEM_LIMIT_BYTES = 56 * 1024 * 1024


def _row_tile(total, candidates=(1024, 512, 384, 256, 128)):
    for c in candidates:
        if total % c == 0:
            return c
    raise ValueError(f"no row tile for {total}")


def _rms(x, g):
    return x * lax.rsqrt(jnp.mean(x * x, axis=-1, keepdims=True) + RMS_EPS) * g


def _dot(a, b):
    return jnp.dot(a, b, preferred_element_type=F32)


def _dot_nt(a, b):
    return lax.dot_general(a, b, (((1,), (1,)), ((), ())), preferred_element_type=F32)


def _dot_tn(a, b):
    return lax.dot_general(a, b, (((0,), (0,)), ((), ())), preferred_element_type=F32)


def _sigmoid(x):
    return 1.0 / (1.0 + jnp.exp(-x))


def _log_sigmoid(x):
    return jnp.minimum(x, 0.0) - jnp.log(1.0 + jnp.exp(-jnp.abs(x)))


def _run_interleaved(chains):
    results = [None] * len(chains)
    live = list(range(len(chains)))
    while live:
        for idx in list(live):
            try:
                next(chains[idx])
            except StopIteration as done:
                results[idx] = done.value
                live.remove(idx)
    return results


def _const_spec(shape):
    return pl.BlockSpec(shape, lambda *_: (0,) * len(shape), pipeline_mode=pl.Buffered(1))


def _params(n_axes):
    return pltpu.CompilerParams(dimension_semantics=("arbitrary",) * n_axes,
                                vmem_limit_bytes=VMEM_LIMIT_BYTES)


def _ffn_body(*refs, n_pieces, pieces_per_seq, with_mix):
    pieces, rest = refs[:n_pieces], refs[n_pieces:]
    head_ref = None
    if pieces_per_seq is not None:
        head_ref, rest = rest[0], rest[1:]
    mix_refs = None
    if with_mix:
        mix_refs, rest = rest[:3], rest[3:]
    g0_ref, g1_ref, win_ref, wout_ref, o_ref = rest
    tm = o_ref.shape[0]
    piece_rows = tm // n_pieces

    def rows_of(lo, hi):
        if n_pieces == 1:
            return pieces[0][lo:hi, :]
        parts = []
        for r in range(lo // piece_rows, hi // piece_rows):
            part = pieces[r][...]
            if head_ref is not None:
                is_head = (pl.program_id(0) * n_pieces + r) % pieces_per_seq == 0
                part = jnp.where(is_head, head_ref[...], part)
            parts.append(part)
        return parts[0] if len(parts) == 1 else jnp.concatenate(parts, axis=0)

    def chain(lo, hi):
        h = rows_of(lo, hi)
        if mix_refs is not None:
            m_ref, gm_ref, wm_ref = mix_refs
            h = h + _rms(_dot(m_ref[lo:hi, :], wm_ref[...]), gm_ref[...])
        xb = _rms(h, g0_ref[...]).astype(BF16)
        yield
        gate = _dot(xb, win_ref[:, :D_FF])
        up = _dot(xb, win_ref[:, D_FF:])
        a = (gate * _sigmoid(gate) * up).astype(BF16)
        yield
        y = _dot(a, wout_ref[...])
        yield
        o_ref[lo:hi, :] = h + 0.5 * _rms(y, g1_ref[...])

    align = 16 if n_pieces == 1 else piece_rows
    n_split = next(c for c in (3, 2, 1) if tm % (c * align) == 0)
    step = tm // n_split
    _run_interleaved([chain(c * step, (c + 1) * step) for c in range(n_split)])


def _ffn(src, g_pre, g_post, w_in, w_out, layer, which, *, tm, mode="stream", seq_head=None, pieces_per_seq=None,
         mix=None):
    weights = [pl.BlockSpec((None, None, D_MODEL, 2 * D_FF), lambda i: (layer, which, 0, 0),
                            pipeline_mode=pl.Buffered(1)),
               pl.BlockSpec((None, None, D_FF, D_MODEL), lambda i: (layer, which, 0, 0),
                            pipeline_mode=pl.Buffered(1))]
    gains = [_const_spec((1, D_MODEL)), _const_spec((1, D_MODEL))]
    if mode == "stream":
        n, out_rows = 1, src.shape[0]
        pieces, extra, extra_specs = [pl.BlockSpec((tm, D_MODEL), lambda i: (i, 0))], [], []
        if mix is not None:
            extra = list(mix)
            extra_specs = [pl.BlockSpec((tm, mix[0].shape[1]), lambda i: (i, 0)), _const_spec((1, D_MODEL)),
                           _const_spec(mix[2].shape)]
    else:
        n = tm // SB_BLOCK
        x_per_seq = pieces_per_seq - 1
        if mode == "enter":
            index = lambda i, r: jnp.maximum(i * n + r - (i * n + r) // pieces_per_seq - 1, 0)
            out_rows = src.shape[0] // x_per_seq * pieces_per_seq
            extra, extra_specs = [seq_head], [_const_spec((SB_BLOCK, D_MODEL))]
        else:
            index = lambda i, r: i * n + r + (i * n + r) // x_per_seq + 1
            out_rows = src.shape[0] // pieces_per_seq * x_per_seq
            extra, extra_specs = [], []
        pieces = [pl.BlockSpec((SB_BLOCK, D_MODEL), functools.partial(lambda i, r: (index(i, r), 0), r=r))
                  for r in range(n)]
    return pl.pallas_call(
        functools.partial(_ffn_body, n_pieces=n, pieces_per_seq=pieces_per_seq if mode == "enter" else None,
                          with_mix=mix is not None),
        grid=(out_rows // tm,),
        in_specs=pieces + extra_specs + gains + weights,
        out_specs=pl.BlockSpec((tm, D_MODEL), lambda i: (i, 0)),
        out_shape=jax.ShapeDtypeStruct((out_rows, D_MODEL), F32),
        compiler_params=_params(1),
        name="ffn",
    )(*([src] * n), *extra, g_pre, g_post, w_in, w_out)


def _conv_body(h_ref, halo_ref, g2_ref, g3_ref, cw_ref, win_ref, wout_ref, o_ref, *, tm, lp, n_split):
    i = pl.program_id(0)
    g2 = g2_ref[...]
    cw = cw_ref[...]
    rows = tm // n_split
    row = lax.broadcasted_iota(jnp.int32, (rows, 1), 0)

    def gated(x_bf16, pos):
        pc = _dot(x_bf16, win_ref[:, D_MODEL:2 * D_MODEL])
        ph = _dot(x_bf16, win_ref[:, 2 * D_MODEL:])
        return jnp.where(pos >= PAD, pc * ph, 0.0)

    hrow = lax.broadcasted_iota(jnp.int32, (8, 1), 0)
    before = [gated(_rms(halo_ref[...], g2).astype(BF16), (i * tm - 8 + hrow + lp) % lp)]

    def chain(c):
        lo = c * rows
        h = h_ref[lo:lo + rows, :]
        xb = _rms(h, g2).astype(BF16)
        yield
        uu = gated(xb, (i * tm + lo + row) % lp)
        before.append(uu[rows - 8:])
        yield
        tail = before[c]
        prev1 = jnp.where(row == 0, tail[7:8], pltpu.roll(uu, 1, 0))
        prev2 = jnp.where(row == 0, tail[6:7], jnp.where(row == 1, tail[7:8], pltpu.roll(uu, 2, 0)))
        conv = cw[2:3] * uu + cw[1:2] * prev1 + cw[0:1] * prev2
        pb = _dot(xb, win_ref[:, :D_MODEL])
        y = _dot((pb * conv).astype(BF16), wout_ref[...])
        yield
        o_ref[lo:lo + rows, :] = h + _rms(y, g3_ref[...])

    _run_interleaved([chain(c) for c in range(n_split)])


def _conv_mixer(h, g_pre, g_post, conv_w, w_in, w_out, *, tm, lp):
    t = h.shape[0]
    row = pl.BlockSpec((tm, D_MODEL), lambda i: (i, 0))
    halo = pl.BlockSpec((8, D_MODEL), lambda i: (jnp.maximum(i * (tm // 8) - 1, 0), 0))
    return pl.pallas_call(
        functools.partial(_conv_body, tm=tm, lp=lp, n_split=next(c for c in (3, 2, 1) if tm % (16 * c) == 0)),
        grid=(t // tm,),
        in_specs=[row, halo, _const_spec((1, D_MODEL)), _const_spec((1, D_MODEL)),
                  _const_spec((8, D_MODEL)), _const_spec((D_MODEL, 3 * D_MODEL)),
                  _const_spec((D_MODEL, D_MODEL))],
        out_specs=row,
        out_shape=jax.ShapeDtypeStruct(h.shape, F32),
        compiler_params=_params(1),
        name="conv_mixer",
    )(h, h, g_pre, g_post, conv_w, w_in, w_out)


def _split3(x):
    hi = x.astype(BF16)
    r = x - hi.astype(F32)
    mid = r.astype(BF16)
    lo = (r - mid.astype(F32)).astype(BF16)
    return hi, mid, lo


def _gla_chunk_head(q, k, v, b, s_t):
    n_sub = GLA_CHUNK // GLA_SUB
    sub_row = lax.broadcasted_iota(jnp.int32, (GLA_SUB, 1), 0)
    col = lax.broadcasted_iota(jnp.int32, (1, GLA_CHUNK), 1)
    att_rows = []
    for i in range(n_sub):
        r0 = i * GLA_SUB
        qd, kd, bd = q[r0:r0 + GLA_SUB], k[r0:r0 + GLA_SUB], b[r0:r0 + GLA_SUB]
        blk = jnp.zeros((GLA_SUB, GLA_CHUNK), F32)
        if i > 0:
            b_ref = b[r0 - 1:r0]
            q_t = (qd * jnp.exp(bd - b_ref)).astype(BF16)
            k_t = (k * jnp.exp(jnp.minimum(b_ref - b, 0.0))).astype(BF16)
            blk = jnp.where(col < r0, _dot_nt(q_t, k_t), 0.0)
        for s in range(GLA_SUB):
            e = jnp.exp(jnp.where(sub_row >= s, bd - bd[s:s + 1], -jnp.inf))
            a_col = jnp.sum(qd * e * kd[s:s + 1], axis=-1, keepdims=True)
            blk = jnp.where(col == r0 + s, a_col, blk)
        att_rows.append(blk)
    att = jnp.concatenate(att_rows, axis=0)
    o = _dot(att.astype(BF16), v.astype(BF16))
    o = o + _dot_nt((q * jnp.exp(b)).astype(BF16), s_t.astype(BF16))
    b_last = b[GLA_CHUNK - 1:GLA_CHUNK]
    k_dec = (k * jnp.exp(b_last - b)).astype(BF16)
    s_new = s_t * jnp.exp(b_last) + _dot_tn(v.astype(BF16), k_dec)
    return o, s_new


def _gla_chunk_mild(heads):
    t_row = lax.broadcasted_iota(jnp.int32, (GLA_CHUNK, 1), 0)
    s_col = lax.broadcasted_iota(jnp.int32, (1, GLA_CHUNK), 1)
    staged = []
    for q, k, v, b, s_t in heads:
        q_dec = (q * jnp.exp(b)).astype(BF16)
        att = _dot_nt(q_dec, (k * jnp.exp(-b)).astype(BF16))
        o_state = _dot_nt(q_dec, s_t.astype(BF16))
        b_last = b[GLA_CHUNK - 1:GLA_CHUNK]
        vb = v.astype(BF16)
        s_new = s_t * jnp.exp(b_last) + _dot_tn(vb, (k * jnp.exp(b_last - b)).astype(BF16))
        staged.append((att, o_state, vb, s_new))
    out = []
    for att, o_state, vb, s_new in staged:
        att = jnp.where(s_col <= t_row, att, 0.0).astype(BF16)
        out.append((_dot(att, vb) + o_state, s_new))
    return out


def _gla_body(h_ref, g2_ref, g3_ref, win_ref, wa_ref, wgu_ref, bg_ref, gn_ref, wout_ref, tri_ref,
              o_ref, s_ref, p_ref, b_ref, oc_ref, mild_ref, *, tb):
    jb = pl.program_id(1)

    @pl.when(jb == 0)
    def _():
        s_ref[...] = jnp.zeros_like(s_ref)

    half = tb // 2 if tb % (2 * GLA_CHUNK) == 0 else tb
    tri = tri_ref[...]

    def project(lo):
        ub = _rms(h_ref[0, lo:lo + half, :], g2_ref[...]).astype(BF16)
        yield
        p = _dot(ub, win_ref[...])
        a = _dot(ub, wa_ref[...]).astype(BF16)
        pos = jb * tb + lo + lax.broadcasted_iota(jnp.int32, (half, 1), 0)
        valid = pos >= PAD
        rows = slice(lo, lo + half)
        p_ref[rows, :GLA_DK] = p[:, :GLA_DK] * (GLA_HK ** -0.5)
        p_ref[rows, GLA_DK:2 * GLA_DK] = jnp.where(valid, p[:, GLA_DK:2 * GLA_DK], 0.0)
        p_ref[rows, 2 * GLA_DK:2 * GLA_DK + GLA_DV] = jnp.where(valid, p[:, 2 * GLA_DK:2 * GLA_DK + GLA_DV], 0.0)
        p_ref[rows, 2 * GLA_DK + GLA_DV:] = p[:, 2 * GLA_DK + GLA_DV:]
        yield
        x = _dot(a, wgu_ref[...]) + bg_ref[...]
        la = _log_sigmoid(x) * (1.0 / GLA_GATE_NORMALIZER)
        for c in range(half // GLA_CHUNK):
            b_c = sum(_dot(tri, part) for part in _split3(la[c * GLA_CHUNK:(c + 1) * GLA_CHUNK]))
            b_ref[lo + c * GLA_CHUNK:lo + (c + 1) * GLA_CHUNK, :] = b_c
            mild_ref[lo // GLA_CHUNK + c] = (jnp.min(b_c[GLA_CHUNK - 1:]) > -GLA_MILD_DECAY).astype(jnp.int32)

    _run_interleaved([project(lo) for lo in range(0, tb, half)])

    def chunk(c, carry):
        r0 = pl.multiple_of(c * GLA_CHUNK, GLA_CHUNK)
        rows = pl.ds(r0, GLA_CHUNK)
        b_all = b_ref[rows, :]
        mild = mild_ref[c] > 0

        def run(chunk_fn):
            heads = []
            for hh in range(GLA_HEADS):
                kq = slice(hh * GLA_HK, (hh + 1) * GLA_HK)
                q = p_ref[rows, kq]
                k = p_ref[rows, GLA_DK + hh * GLA_HK:GLA_DK + (hh + 1) * GLA_HK]
                v = p_ref[rows, 2 * GLA_DK + hh * GLA_HV:2 * GLA_DK + (hh + 1) * GLA_HV]
                heads.append((q, k, v, b_all[:, kq], s_ref[hh]))
            for hh, (o, s_new) in enumerate(chunk_fn(heads)):
                s_ref[hh] = s_new
                oc_ref[rows, hh * GLA_HV:(hh + 1) * GLA_HV] = o

        pl.when(mild)(functools.partial(run, _gla_chunk_mild))
        pl.when(jnp.logical_not(mild))(
            functools.partial(run, lambda heads: [_gla_chunk_head(*head) for head in heads]))
        return carry

    lax.fori_loop(0, tb // GLA_CHUNK, chunk, 0)

    gn = gn_ref[...]

    def finish(lo):
        rows = slice(lo, lo + half)
        normed = []
        for hh in range(GLA_HEADS):
            oh = oc_ref[rows, hh * GLA_HV:(hh + 1) * GLA_HV]
            normed.append(oh * lax.rsqrt(jnp.mean(oh * oh, axis=-1, keepdims=True) + RMS_EPS) * gn)
        gate = p_ref[rows, 2 * GLA_DK + GLA_DV:]
        y = (jnp.concatenate(normed, axis=-1) * (gate * _sigmoid(gate))).astype(BF16)
        yield
        z = _dot(y, wout_ref[...])
        yield
        o_ref[0, rows, :] = h_ref[0, rows, :] + _rms(z, g3_ref[...])

    _run_interleaved([finish(lo) for lo in range(0, tb, half)])


def _gla_mixer(h3, g_pre, g_post, w_in, w_a, w_gu, b_gate, g_norm, w_out, tri, *, tb):
    bsz, lp, _ = h3.shape
    blk = pl.BlockSpec((1, tb, D_MODEL), lambda b, j: (b, j, 0))
    n_in = 2 * GLA_DK + 2 * GLA_DV
    return pl.pallas_call(
        functools.partial(_gla_body, tb=tb),
        grid=(bsz, lp // tb),
        in_specs=[blk, _const_spec((1, D_MODEL)), _const_spec((1, D_MODEL)),
                  _const_spec((D_MODEL, n_in)), _const_spec((D_MODEL, LANES)),
                  _const_spec((LANES, GLA_DK)), _const_spec((1, GLA_DK)), _const_spec((1, GLA_HV)),
                  _const_spec((GLA_DV, D_MODEL)), _const_spec((GLA_CHUNK, GLA_CHUNK))],
        out_specs=blk,
        out_shape=jax.ShapeDtypeStruct(h3.shape, F32),
        scratch_shapes=[pltpu.VMEM((GLA_HEADS, GLA_HV, GLA_HK), F32),
                        pltpu.VMEM((tb, n_in), F32),
                        pltpu.VMEM((tb, GLA_DK), F32),
                        pltpu.VMEM((tb, GLA_DV), F32),
                        pltpu.SMEM((tb // GLA_CHUNK,), jnp.int32)],
        compiler_params=_params(2),
        name="gla_mixer",
    )(h3, g_pre, g_post, w_in, w_a, w_gu, b_gate, g_norm, w_out, tri)


def _proj_in_body(h_ref, g_ref, w_ref, o_ref):
    o_ref[...] = _dot(_rms(h_ref[...], g_ref[...]).astype(BF16), w_ref[...]).astype(o_ref.dtype)


def _proj_in(h, g, w, *, tm):
    t, n = h.shape[0], w.shape[1]
    return pl.pallas_call(
        _proj_in_body,
        grid=(t // tm,),
        in_specs=[pl.BlockSpec((tm, D_MODEL), lambda i: (i, 0)), _const_spec((1, D_MODEL)),
                  _const_spec(w.shape)],
        out_specs=pl.BlockSpec((tm, n), lambda i: (i, 0)),
        out_shape=jax.ShapeDtypeStruct((t, n), BF16),
        compiler_params=_params(1),
        name="proj_in",
    )(h, g, w)


def _proj_out_body(h_ref, x_ref, g_ref, w_ref, o_ref):
    o_ref[...] = h_ref[...] + _rms(_dot(x_ref[...], w_ref[...]), g_ref[...])


def _proj_out(h, x, g, w, *, tm):
    t, kdim = x.shape
    row = pl.BlockSpec((tm, D_MODEL), lambda i: (i, 0))
    return pl.pallas_call(
        _proj_out_body,
        grid=(t // tm,),
        in_specs=[row, pl.BlockSpec((tm, kdim), lambda i: (i, 0)), _const_spec((1, D_MODEL)),
                  _const_spec(w.shape)],
        out_specs=row,
        out_shape=jax.ShapeDtypeStruct(h.shape, F32),
        compiler_params=_params(1),
        name="proj_out",
    )(h, x, g, w)


SB_WIDE = 3
SB_DEAD = -104.0
SB_GROUP = 3


def _sb_scores(qs, kwin, mask):
    z = _dot_nt(qs, kwin)
    ls = _log_sigmoid(z)
    return ls, jnp.where(mask, ls - z, 0.0)


def _sb_suffix(l1m, u):
    hi = l1m.astype(BF16)
    lo = (l1m - hi.astype(F32)).astype(BF16)
    return _dot(jnp.concatenate([hi, lo], axis=1), u)


def _sb_body(q_ref, k_ref, v_ref, u_ref, o_ref, *, nq):
    lane = lax.broadcasted_iota(jnp.int32, (1, LANES), 1)
    low = lane < SB_HEAD_DIM
    zero = jnp.zeros((), BF16)
    row = lax.broadcasted_iota(jnp.int32, (2 * SB_BLOCK, 1), 0) % SB_BLOCK
    col = lax.broadcasted_iota(jnp.int32, (1, SB_BLOCK), 1)
    u = u_ref[...]

    def stacked_q(m):
        q = q_ref[0, pl.ds(pl.multiple_of(m * SB_BLOCK, SB_BLOCK), SB_BLOCK), :] * (SB_HEAD_DIM ** -0.5)
        return jnp.concatenate([jnp.where(low, q, zero), jnp.where(low, zero, q)], axis=0)

    def wide(m, near_start):
        qs = stacked_q(m)
        t_glob = m * SB_BLOCK + row
        kparts, vparts, masks = [], [], []
        for bi in range(SB_WIDE):
            jv = m - (SB_WIDE - 1) + bi
            k0 = pl.multiple_of(jnp.maximum(jv, 0) * SB_BLOCK, SB_BLOCK)
            kparts.append(k_ref[0, pl.ds(k0, SB_BLOCK), :])
            vparts.append(v_ref[0, pl.ds(k0, SB_BLOCK), :])
            s_glob = jv * SB_BLOCK + col
            causal = s_glob < t_glob if bi == SB_WIDE - 1 else None
            if near_start:
                real = jnp.broadcast_to(s_glob >= PAD, (2 * SB_BLOCK, SB_BLOCK))
                causal = real if causal is None else causal & real
            masks.append(causal)
        z = _dot_nt(qs, jnp.concatenate(kparts, axis=0))
        yield
        lsp, cs = [], []
        for bi in range(SB_WIDE):
            zb = z[:, bi * SB_BLOCK:(bi + 1) * SB_BLOCK]
            ls = _log_sigmoid(zb)
            l1m = ls - zb
            if masks[bi] is not None:
                l1m = jnp.where(masks[bi], l1m, 0.0)
            lsp.append(ls)
            cs.append(_sb_suffix(l1m, u))
        yield
        ws, later = [], None
        for bi in reversed(range(SB_WIDE)):
            tail = cs[bi][:, :SB_BLOCK]
            w = jnp.exp(lsp[bi] + (tail if later is None else tail + later))
            ws.append(w if masks[bi] is None else jnp.where(masks[bi], w, 0.0))
            tot = cs[bi][:, SB_BLOCK:]
            later = tot if later is None else later + tot
        w = jnp.concatenate(ws[::-1], axis=1)
        acc = _dot(w.astype(BF16), jnp.concatenate(vparts, axis=0))
        return later, acc

    def alive(c):
        return (jnp.max(c) >= SB_DEAD).astype(jnp.int32)

    def finish(m, car, acc):
        def cond(st):
            return (st[0] >= 0) & (st[1] > 0)

        def body(st):
            j, _, car, acc = st
            k0 = pl.multiple_of(j * SB_BLOCK, SB_BLOCK)
            mask = jnp.broadcast_to(k0 + col >= PAD, (2 * SB_BLOCK, SB_BLOCK))
            ls, l1m = _sb_scores(stacked_q(m), k_ref[0, pl.ds(k0, SB_BLOCK), :], mask)
            cs = _sb_suffix(l1m, u)
            w = jnp.where(mask, jnp.exp(ls + cs[:, :SB_BLOCK] + car), 0.0)
            acc = acc + _dot(w.astype(BF16), v_ref[0, pl.ds(k0, SB_BLOCK), :])
            car = car + cs[:, SB_BLOCK:]
            return j - 1, alive(car), car, acc

        _, _, _, acc = lax.while_loop(cond, body, (m - SB_WIDE, alive(car), car, acc))
        put(m, acc)

    def put(m, acc):
        out = jnp.where(low, acc[:SB_BLOCK], acc[SB_BLOCK:])
        o_ref[0, pl.ds(pl.multiple_of(m * SB_BLOCK, SB_BLOCK), SB_BLOCK), :] = out.astype(o_ref.dtype)

    def group(ms, near_start):
        if not ms:
            return
        states = _run_interleaved([wide(m, near_start) for m in ms])
        for m, (_, acc) in zip(ms, states):
            put(m, acc)
        worst = functools.reduce(jnp.maximum, [car for car, _ in states])
        more = (jnp.max(worst) >= SB_DEAD) & (ms[-1] >= SB_WIDE)

        @pl.when(more)
        def _():
            for m, (car, acc) in zip(ms, states):
                finish(m, car, acc)

    n_start = -(-SB_WIDE // SB_GROUP)
    for gi in range(n_start):
        group([jnp.int32(m) for m in range(gi * SB_GROUP, (gi + 1) * SB_GROUP) if m < nq], True)

    def steady(gi, carry):
        group([gi * SB_GROUP + r for r in range(SB_GROUP)], False)
        return carry

    lax.fori_loop(n_start, nq // SB_GROUP, steady, 0)
    group([jnp.int32(m) for m in range(max(nq // SB_GROUP, n_start) * SB_GROUP, nq)], False)


def _sb_attention(qkv3, u2):
    bsz, lp, _ = qkv3.shape
    n_pairs = D_MODEL // LANES
    assert lp >= SB_WIDE * SB_BLOCK

    def col_spec(offset):
        return pl.BlockSpec((1, lp, LANES), lambda b, p: (b, 0, offset + p))

    return pl.pallas_call(
        functools.partial(_sb_body, nq=lp // SB_BLOCK),
        grid=(bsz, n_pairs),
        in_specs=[col_spec(0), col_spec(n_pairs), col_spec(2 * n_pairs),
                  _const_spec((2 * SB_BLOCK, 2 * SB_BLOCK))],
        out_specs=pl.BlockSpec((1, lp, LANES), lambda b, p: (b, 0, p)),
        out_shape=jax.ShapeDtypeStruct((bsz, lp, D_MODEL), BF16),
        compiler_params=_params(2),
        name="sb_attention",
    )(qkv3, qkv3, qkv3, u2)


def _suffix_sum_matrix():
    j = jnp.arange(SB_BLOCK)[:, None]
    s = jnp.arange(2 * SB_BLOCK)[None, :]
    u = ((j > s) | (s >= SB_BLOCK)).astype(BF16)
    return jnp.concatenate([u, u], axis=0)


def kernel(x, meta_tokens, norm_gains, ffn_w_in, ffn_w_out, conv_w_in, conv_w, conv_w_out,
           gla_w_in, gla_w_gate_up, gla_b_gate, gla_norm, gla_w_out, sb_w_in, sb_w_out):
    bsz, seq, _ = x.shape
    depth = norm_gains.shape[0]
    lp = PAD + N_META + seq
    assert lp % SB_BLOCK == 0 and lp % GLA_CHUNK == 0
    t = bsz * lp
    tm = _row_tile(t, (512, 384, 256, 128))
    tm_ffn = _row_tile(t, (768, 512, 384, 256, 128))
    tb = _row_tile(lp, (384, 256, 128, 64))

    seq_head = jnp.concatenate([jnp.zeros((PAD, D_MODEL), x.dtype), meta_tokens.astype(x.dtype)], axis=0)
    pps = lp // SB_BLOCK
    tm_out = _row_tile(bsz * seq, (512, 256, 128))

    gains = norm_gains.reshape(depth, -1, 1, D_MODEL)
    ffn_w_in, ffn_w_out = ffn_w_in.astype(BF16), ffn_w_out.astype(BF16)
    tri = (jnp.arange(GLA_CHUNK)[:, None] >= jnp.arange(GLA_CHUNK)[None, :]).astype(BF16)
    u2 = _suffix_sum_matrix()

    h = x.reshape(bsz * seq, D_MODEL)
    for i in range(depth):
        g = gains[i]
        if i == 0:
            h = _ffn(h, g[0], g[1], ffn_w_in, ffn_w_out, i, 0, tm=tm_ffn, mode="enter", seq_head=seq_head,
                     pieces_per_seq=pps)
        else:
            h = _ffn(h, g[0], g[1], ffn_w_in, ffn_w_out, i, 0, tm=tm_ffn)
        kind, j = i % 3, i // 3
        if kind == 0:
            cw = jnp.pad(conv_w[j], ((0, 8 - CONV_WIDTH), (0, 0)))
            h = _conv_mixer(h, g[2], g[3], cw, conv_w_in[j].astype(BF16), conv_w_out[j].astype(BF16),
                            tm=tm_ffn, lp=lp)
        elif kind == 1:
            n_main = 2 * GLA_DK + 2 * GLA_DV
            w_a = jnp.pad(gla_w_in[j][:, n_main:], ((0, 0), (0, LANES - GLA_RANK))).astype(BF16)
            w_gu = jnp.pad(gla_w_gate_up[j], ((0, LANES - GLA_RANK), (0, 0))).astype(BF16)
            h = _gla_mixer(h.reshape(bsz, lp, D_MODEL), g[2], g[3], gla_w_in[j][:, :n_main].astype(BF16),
                           w_a, w_gu, gla_b_gate[j][None], gla_norm[j][None], gla_w_out[j].astype(BF16),
                           tri, tb=tb).reshape(t, D_MODEL)
        else:
            qkv = _proj_in(h, g[2], sb_w_in[j].astype(BF16), tm=tm)
            o = _sb_attention(qkv.reshape(bsz, lp, 3 * D_MODEL), u2)
            mix = (o.reshape(t, D_MODEL), g[3], sb_w_out[j].astype(BF16))
            if i == depth - 1:
                h = _proj_out(h, *mix, tm=tm)
        if i == depth - 1:
            h = _ffn(h, g[4], g[5], ffn_w_in, ffn_w_out, i, 1, tm=tm_out, mode="leave", pieces_per_seq=pps)
        else:
            h = _ffn(h, g[4], g[5], ffn_w_in, ffn_w_out, i, 1, tm=tm_ffn, mix=mix if kind == 2 else None)
    return h.reshape(bsz, seq, D_MODEL)
```

```python
import functools

import jax
import jax.numpy as jnp
from jax import lax
from jax.experimental import pallas as pl
from jax.experimental.pallas import tpu as pltpu

F32 = jnp.float32
BF16 = jnp.bfloat16

D_MODEL = 1024
D_FF = 2816
N_META = 16
RMS_EPS = 1e-6
CONV_WIDTH = 3
GLA_HEADS = 4
GLA_HK = 128
GLA_HV = 256
GLA_DK = GLA_HEADS * GLA_HK
GLA_DV = GLA_HEADS * GLA_HV
GLA_RANK = 16
GLA_GATE_NORMALIZER = 16.0
GLA_CHUNK = 64
GLA_SUB = 16
GLA_MILD_DECAY = 40.0
SB_HEAD_DIM = 64
SB_BLOCK = 128
LANES = 128
PAD = SB_BLOCK - N_META

VMEM_LIMIT_BYTES = 56 * 1024 * 1024


def _row_tile(total, candidates=(1024, 512, 384, 256, 128)):
    for c in candidates:
        if total % c == 0:
            return c
    raise ValueError(f"no row tile for {total}")


def _rms(x, g):
    return x * lax.rsqrt(jnp.mean(x * x, axis=-1, keepdims=True) + RMS_EPS) * g


def _dot(a, b):
    return jnp.dot(a, b, preferred_element_type=F32)


def _dot_nt(a, b):
    return lax.dot_general(a, b, (((1,), (1,)), ((), ())), preferred_element_type=F32)


def _dot_tn(a, b):
    return lax.dot_general(a, b, (((0,), (0,)), ((), ())), preferred_element_type=F32)


def _sigmoid(x):
    return 1.0 / (1.0 + jnp.exp(-x))


def _log_sigmoid(x):
    return jnp.minimum(x, 0.0) - jnp.log(1.0 + jnp.exp(-jnp.abs(x)))


def _run_interleaved(chains):
    results = [None] * len(chains)
    live = list(range(len(chains)))
    while live:
        for idx in list(live):
            try:
                next(chains[idx])
            except StopIteration as done:
                results[idx] = done.value
                live.remove(idx)
    return results


def _const_spec(shape):
    return pl.BlockSpec(shape, lambda *_: (0,) * len(shape), pipeline_mode=pl.Buffered(1))


def _params(n_axes):
    return pltpu.CompilerParams(dimension_semantics=("arbitrary",) * n_axes,
                                vmem_limit_bytes=VMEM_LIMIT_BYTES)


def _ffn_body(*refs, n_pieces, pieces_per_seq, with_mix):
    pieces, rest = refs[:n_pieces], refs[n_pieces:]
    head_ref = None
    if pieces_per_seq is not None:
        head_ref, rest = rest[0], rest[1:]
    mix_refs = None
    if with_mix:
        mix_refs, rest = rest[:3], rest[3:]
    g0_ref, g1_ref, win_ref, wout_ref, o_ref = rest
    tm = o_ref.shape[0]
    piece_rows = tm // n_pieces

    def rows_of(lo, hi):
        if n_pieces == 1:
            return pieces[0][lo:hi, :]
        parts = []
        for r in range(lo // piece_rows, hi // piece_rows):
            part = pieces[r][...]
            if head_ref is not None:
                is_head = (pl.program_id(0) * n_pieces + r) % pieces_per_seq == 0
                part = jnp.where(is_head, head_ref[...], part)
            parts.append(part)
        return parts[0] if len(parts) == 1 else jnp.concatenate(parts, axis=0)

    def chain(lo, hi):
        h = rows_of(lo, hi)
        if mix_refs is not None:
            m_ref, gm_ref, wm_ref = mix_refs
            h = h + _rms(_dot(m_ref[lo:hi, :], wm_ref[...]), gm_ref[...])
        xb = _rms(h, g0_ref[...]).astype(BF16)
        yield
        gate = _dot(xb, win_ref[:, :D_FF])
        up = _dot(xb, win_ref[:, D_FF:])
        a = (gate * _sigmoid(gate) * up).astype(BF16)
        yield
        y = _dot(a, wout_ref[...])
        yield
        o_ref[lo:hi, :] = h + 0.5 * _rms(y, g1_ref[...])

    align = 16 if n_pieces == 1 else piece_rows
    n_split = next(c for c in (3, 2, 1) if tm % (c * align) == 0)
    step = tm // n_split
    _run_interleaved([chain(c * step, (c + 1) * step) for c in range(n_split)])


def _ffn(src, g_pre, g_post, w_in, w_out, layer, which, *, tm, mode="stream", seq_head=None, pieces_per_seq=None,
         mix=None):
    weights = [pl.BlockSpec((None, None, D_MODEL, 2 * D_FF), lambda i: (layer, which, 0, 0),
                            pipeline_mode=pl.Buffered(1)),
               pl.BlockSpec((None, None, D_FF, D_MODEL), lambda i: (layer, which, 0, 0),
                            pipeline_mode=pl.Buffered(1))]
    gains = [_const_spec((1, D_MODEL)), _const_spec((1, D_MODEL))]
    if mode == "stream":
        n, out_rows = 1, src.shape[0]
        pieces, extra, extra_specs = [pl.BlockSpec((tm, D_MODEL), lambda i: (i, 0))], [], []
        if mix is not None:
            extra = list(mix)
            extra_specs = [pl.BlockSpec((tm, mix[0].shape[1]), lambda i: (i, 0)), _const_spec((1, D_MODEL)),
                           _const_spec(mix[2].shape)]
    else:
        n = tm // SB_BLOCK
        x_per_seq = pieces_per_seq - 1
        if mode == "enter":
            index = lambda i, r: jnp.maximum(i * n + r - (i * n + r) // pieces_per_seq - 1, 0)
            out_rows = src.shape[0] // x_per_seq * pieces_per_seq
            extra, extra_specs = [seq_head], [_const_spec((SB_BLOCK, D_MODEL))]
        else:
            index = lambda i, r: i * n + r + (i * n + r) // x_per_seq + 1
            out_rows = src.shape[0] // pieces_per_seq * x_per_seq
            extra, extra_specs = [], []
        pieces = [pl.BlockSpec((SB_BLOCK, D_MODEL), functools.partial(lambda i, r: (index(i, r), 0), r=r))
                  for r in range(n)]
    return pl.pallas_call(
        functools.partial(_ffn_body, n_pieces=n, pieces_per_seq=pieces_per_seq if mode == "enter" else None,
                          with_mix=mix is not None),
        grid=(out_rows // tm,),
        in_specs=pieces + extra_specs + gains + weights,
        out_specs=pl.BlockSpec((tm, D_MODEL), lambda i: (i, 0)),
        out_shape=jax.ShapeDtypeStruct((out_rows, D_MODEL), F32),
        compiler_params=_params(1),
        name="ffn",
    )(*([src] * n), *extra, g_pre, g_post, w_in, w_out)


def _conv_body(h_ref, halo_ref, g2_ref, g3_ref, cw_ref, win_ref, wout_ref, o_ref, *, tm, lp, n_split):
    i = pl.program_id(0)
    g2 = g2_ref[...]
    cw = cw_ref[...]
    rows = tm // n_split
    row = lax.broadcasted_iota(jnp.int32, (rows, 1), 0)

    def gated(x_bf16, pos):
        pc = _dot(x_bf16, win_ref[:, D_MODEL:2 * D_MODEL])
        ph = _dot(x_bf16, win_ref[:, 2 * D_MODEL:])
        return jnp.where(pos >= PAD, pc * ph, 0.0)

    hrow = lax.broadcasted_iota(jnp.int32, (8, 1), 0)
    before = [gated(_rms(halo_ref[...], g2).astype(BF16), (i * tm - 8 + hrow + lp) % lp)]

    def chain(c):
        lo = c * rows
        h = h_ref[lo:lo + rows, :]
        xb = _rms(h, g2).astype(BF16)
        yield
        uu = gated(xb, (i * tm + lo + row) % lp)
        before.append(uu[rows - 8:])
        yield
        tail = before[c]
        prev1 = jnp.where(row == 0, tail[7:8], pltpu.roll(uu, 1, 0))
        prev2 = jnp.where(row == 0, tail[6:7], jnp.where(row == 1, tail[7:8], pltpu.roll(uu, 2, 0)))
        conv = cw[2:3] * uu + cw[1:2] * prev1 + cw[0:1] * prev2
        pb = _dot(xb, win_ref[:, :D_MODEL])
        y = _dot((pb * conv).astype(BF16), wout_ref[...])
        yield
        o_ref[lo:lo + rows, :] = h + _rms(y, g3_ref[...])

    _run_interleaved([chain(c) for c in range(n_split)])


def _conv_mixer(h, g_pre, g_post, conv_w, w_in, w_out, *, tm, lp):
    t = h.shape[0]
    row = pl.BlockSpec((tm, D_MODEL), lambda i: (i, 0))
    halo = pl.BlockSpec((8, D_MODEL), lambda i: (jnp.maximum(i * (tm // 8) - 1, 0), 0))
    return pl.pallas_call(
        functools.partial(_conv_body, tm=tm, lp=lp, n_split=next(c for c in (3, 2, 1) if tm % (16 * c) == 0)),
        grid=(t // tm,),
        in_specs=[row, halo, _const_spec((1, D_MODEL)), _const_spec((1, D_MODEL)),
                  _const_spec((8, D_MODEL)), _const_spec((D_MODEL, 3 * D_MODEL)),
                  _const_spec((D_MODEL, D_MODEL))],
        out_specs=row,
        out_shape=jax.ShapeDtypeStruct(h.shape, F32),
        compiler_params=_params(1),
        name="conv_mixer",
    )(h, h, g_pre, g_post, conv_w, w_in, w_out)


def _split3(x):
    hi = x.astype(BF16)
    r = x - hi.astype(F32)
    mid = r.astype(BF16)
    lo = (r - mid.astype(F32)).astype(BF16)
    return hi, mid, lo


def _gla_chunk_head(q, k, v, b, s_t):
    n_sub = GLA_CHUNK // GLA_SUB
    sub_row = lax.broadcasted_iota(jnp.int32, (GLA_SUB, 1), 0)
    col = lax.broadcasted_iota(jnp.int32, (1, GLA_CHUNK), 1)
    att_rows = []
    for i in range(n_sub):
        r0 = i * GLA_SUB
        qd, kd, bd = q[r0:r0 + GLA_SUB], k[r0:r0 + GLA_SUB], b[r0:r0 + GLA_SUB]
        blk = jnp.zeros((GLA_SUB, GLA_CHUNK), F32)
        if i > 0:
            b_ref = b[r0 - 1:r0]
            q_t = (qd * jnp.exp(bd - b_ref)).astype(BF16)
            k_t = (k * jnp.exp(jnp.minimum(b_ref - b, 0.0))).astype(BF16)
            blk = jnp.where(col < r0, _dot_nt(q_t, k_t), 0.0)
        for s in range(GLA_SUB):
            e = jnp.exp(jnp.where(sub_row >= s, bd - bd[s:s + 1], -jnp.inf))
            a_col = jnp.sum(qd * e * kd[s:s + 1], axis=-1, keepdims=True)
            blk = jnp.where(col == r0 + s, a_col, blk)
        att_rows.append(blk)
    att = jnp.concatenate(att_rows, axis=0)
    o = _dot(att.astype(BF16), v.astype(BF16))
    o = o + _dot_nt((q * jnp.exp(b)).astype(BF16), s_t.astype(BF16))
    b_last = b[GLA_CHUNK - 1:GLA_CHUNK]
    k_dec = (k * jnp.exp(b_last - b)).astype(BF16)
    s_new = s_t * jnp.exp(b_last) + _dot_tn(v.astype(BF16), k_dec)
    return o, s_new


def _gla_chunks_mild(chunks, states):
    t_row = lax.broadcasted_iota(jnp.int32, (GLA_CHUNK, 1), 0)
    s_col = lax.broadcasted_iota(jnp.int32, (1, GLA_CHUNK), 1)
    staged = []
    for heads in chunks:
        per_head = []
        for q, k, v, b in heads:
            q_dec = (q * jnp.exp(b)).astype(BF16)
            att = _dot_nt(q_dec, (k * jnp.exp(-b)).astype(BF16))
            b_last = b[GLA_CHUNK - 1:GLA_CHUNK]
            vb = v.astype(BF16)
            grow = _dot_tn(vb, (k * jnp.exp(b_last - b)).astype(BF16))
            per_head.append((q_dec, att, vb, jnp.exp(b_last), grow))
        staged.append(per_head)
    from_state = []
    for per_head in staged:
        from_state.append([_dot_nt(q_dec, s_t.astype(BF16)) for (q_dec, *_), s_t in zip(per_head, states)])
        states = [s_t * decay + grow for (_, _, _, decay, grow), s_t in zip(per_head, states)]
    outs = []
    for per_head, o_states in zip(staged, from_state):
        outs.append([_dot(jnp.where(s_col <= t_row, att, 0.0).astype(BF16), vb) + o_state
                     for (_, att, vb, _, _), o_state in zip(per_head, o_states)])
    return outs, states


def _gla_body(h_ref, g2_ref, g3_ref, win_ref, wa_ref, wgu_ref, bg_ref, gn_ref, wout_ref, tri_ref,
              o_ref, s_ref, p_ref, b_ref, oc_ref, mild_ref, *, tb):
    jb = pl.program_id(1)

    @pl.when(jb == 0)
    def _():
        s_ref[...] = jnp.zeros_like(s_ref)

    half = tb // 2 if tb % (2 * GLA_CHUNK) == 0 else tb
    tri = tri_ref[...]

    def project(lo):
        ub = _rms(h_ref[0, lo:lo + half, :], g2_ref[...]).astype(BF16)
        yield
        p = _dot(ub, win_ref[...])
        a = _dot(ub, wa_ref[...]).astype(BF16)
        pos = jb * tb + lo + lax.broadcasted_iota(jnp.int32, (half, 1), 0)
        valid = pos >= PAD
        rows = slice(lo, lo + half)
        p_ref[rows, :GLA_DK] = p[:, :GLA_DK] * (GLA_HK ** -0.5)
        p_ref[rows, GLA_DK:2 * GLA_DK] = jnp.where(valid, p[:, GLA_DK:2 * GLA_DK], 0.0)
        p_ref[rows, 2 * GLA_DK:2 * GLA_DK + GLA_DV] = jnp.where(valid, p[:, 2 * GLA_DK:2 * GLA_DK + GLA_DV], 0.0)
        p_ref[rows, 2 * GLA_DK + GLA_DV:] = p[:, 2 * GLA_DK + GLA_DV:]
        yield
        x = _dot(a, wgu_ref[...]) + bg_ref[...]
        la = _log_sigmoid(x) * (1.0 / GLA_GATE_NORMALIZER)
        for c in range(half // GLA_CHUNK):
            b_c = sum(_dot(tri, part) for part in _split3(la[c * GLA_CHUNK:(c + 1) * GLA_CHUNK]))
            b_ref[lo + c * GLA_CHUNK:lo + (c + 1) * GLA_CHUNK, :] = b_c
            mild_ref[lo // GLA_CHUNK + c] = (jnp.min(b_c[GLA_CHUNK - 1:]) > -GLA_MILD_DECAY).astype(jnp.int32)

    _run_interleaved([project(lo) for lo in range(0, tb, half)])

    n_chunks = tb // GLA_CHUNK
    per_iter = next(c for c in (3, 2, 1) if n_chunks % c == 0)

    def chunk_group(ci, carry):
        starts = [pl.multiple_of((ci * per_iter + r) * GLA_CHUNK, GLA_CHUNK) for r in range(per_iter)]
        mild = functools.reduce(jnp.logical_and, [mild_ref[ci * per_iter + r] > 0 for r in range(per_iter)])

        def load(r0):
            rows = pl.ds(r0, GLA_CHUNK)
            b_all = b_ref[rows, :]
            return [(p_ref[rows, hh * GLA_HK:(hh + 1) * GLA_HK],
                     p_ref[rows, GLA_DK + hh * GLA_HK:GLA_DK + (hh + 1) * GLA_HK],
                     p_ref[rows, 2 * GLA_DK + hh * GLA_HV:2 * GLA_DK + (hh + 1) * GLA_HV],
                     b_all[:, hh * GLA_HK:(hh + 1) * GLA_HK]) for hh in range(GLA_HEADS)]

        def store(r0, outs):
            for hh, o in enumerate(outs):
                oc_ref[pl.ds(r0, GLA_CHUNK), hh * GLA_HV:(hh + 1) * GLA_HV] = o

        @pl.when(mild)
        def _():
            outs, states = _gla_chunks_mild([load(r0) for r0 in starts], [s_ref[hh] for hh in range(GLA_HEADS)])
            for r0, chunk_outs in zip(starts, outs):
                store(r0, chunk_outs)
            for hh, s_new in enumerate(states):
                s_ref[hh] = s_new

        @pl.when(jnp.logical_not(mild))
        def _():
            for r0 in starts:
                outs = []
                for hh, head in enumerate(load(r0)):
                    o, s_new = _gla_chunk_head(*head, s_ref[hh])
                    s_ref[hh] = s_new
                    outs.append(o)
                store(r0, outs)

        return carry

    lax.fori_loop(0, n_chunks // per_iter, chunk_group, 0)

    gn = gn_ref[...]

    def finish(lo):
        rows = slice(lo, lo + half)
        normed = []
        for hh in range(GLA_HEADS):
            oh = oc_ref[rows, hh * GLA_HV:(hh + 1) * GLA_HV]
            normed.append(oh * lax.rsqrt(jnp.mean(oh * oh, axis=-1, keepdims=True) + RMS_EPS) * gn)
        gate = p_ref[rows, 2 * GLA_DK + GLA_DV:]
        y = (jnp.concatenate(normed, axis=-1) * (gate * _sigmoid(gate))).astype(BF16)
        yield
        z = _dot(y, wout_ref[...])
        yield
        o_ref[0, rows, :] = h_ref[0, rows, :] + _rms(z, g3_ref[...])

    _run_interleaved([finish(lo) for lo in range(0, tb, half)])


def _gla_mixer(h3, g_pre, g_post, w_in, w_a, w_gu, b_gate, g_norm, w_out, tri, *, tb):
    bsz, lp, _ = h3.shape
    blk = pl.BlockSpec((1, tb, D_MODEL), lambda b, j: (b, j, 0))
    n_in = 2 * GLA_DK + 2 * GLA_DV
    return pl.pallas_call(
        functools.partial(_gla_body, tb=tb),
        grid=(bsz, lp // tb),
        in_specs=[blk, _const_spec((1, D_MODEL)), _const_spec((1, D_MODEL)),
                  _const_spec((D_MODEL, n_in)), _const_spec((D_MODEL, LANES)),
                  _const_spec((LANES, GLA_DK)), _const_spec((1, GLA_DK)), _const_spec((1, GLA_HV)),
                  _const_spec((GLA_DV, D_MODEL)), _const_spec((GLA_CHUNK, GLA_CHUNK))],
        out_specs=blk,
        out_shape=jax.ShapeDtypeStruct(h3.shape, F32),
        scratch_shapes=[pltpu.VMEM((GLA_HEADS, GLA_HV, GLA_HK), F32),
                        pltpu.VMEM((tb, n_in), F32),
                        pltpu.VMEM((tb, GLA_DK), F32),
                        pltpu.VMEM((tb, GLA_DV), F32),
                        pltpu.SMEM((tb // GLA_CHUNK,), jnp.int32)],
        compiler_params=_params(2),
        name="gla_mixer",
    )(h3, g_pre, g_post, w_in, w_a, w_gu, b_gate, g_norm, w_out, tri)


def _proj_in_body(h_ref, g_ref, w_ref, o_ref):
    tm = h_ref.shape[0]
    n_split = next(c for c in (3, 2, 1) if tm % (16 * c) == 0)
    rows = tm // n_split

    def chain(lo):
        xb = _rms(h_ref[lo:lo + rows, :], g_ref[...]).astype(BF16)
        yield
        o_ref[lo:lo + rows, :] = _dot(xb, w_ref[...]).astype(o_ref.dtype)

    _run_interleaved([chain(c * rows) for c in range(n_split)])


def _proj_in(h, g, w, *, tm):
    t, n = h.shape[0], w.shape[1]
    return pl.pallas_call(
        _proj_in_body,
        grid=(t // tm,),
        in_specs=[pl.BlockSpec((tm, D_MODEL), lambda i: (i, 0)), _const_spec((1, D_MODEL)),
                  _const_spec(w.shape)],
        out_specs=pl.BlockSpec((tm, n), lambda i: (i, 0)),
        out_shape=jax.ShapeDtypeStruct((t, n), BF16),
        compiler_params=_params(1),
        name="proj_in",
    )(h, g, w)


def _proj_out_body(h_ref, x_ref, g_ref, w_ref, o_ref):
    o_ref[...] = h_ref[...] + _rms(_dot(x_ref[...], w_ref[...]), g_ref[...])


def _proj_out(h, x, g, w, *, tm):
    t, kdim = x.shape
    row = pl.BlockSpec((tm, D_MODEL), lambda i: (i, 0))
    return pl.pallas_call(
        _proj_out_body,
        grid=(t // tm,),
        in_specs=[row, pl.BlockSpec((tm, kdim), lambda i: (i, 0)), _const_spec((1, D_MODEL)),
                  _const_spec(w.shape)],
        out_specs=row,
        out_shape=jax.ShapeDtypeStruct(h.shape, F32),
        compiler_params=_params(1),
        name="proj_out",
    )(h, x, g, w)


SB_WIDE = 3
SB_DEAD = -104.0
SB_GROUP = 3


def _sb_scores(qs, kwin, mask):
    z = _dot_nt(qs, kwin)
    ls = _log_sigmoid(z)
    return ls, jnp.where(mask, ls - z, 0.0)


def _sb_suffix(l1m, u):
    hi = l1m.astype(BF16)
    lo = (l1m - hi.astype(F32)).astype(BF16)
    return _dot(jnp.concatenate([hi, lo], axis=1), u)


def _sb_body(q_ref, k_ref, v_ref, u_ref, o_ref, *, nq):
    lane = lax.broadcasted_iota(jnp.int32, (1, LANES), 1)
    low = lane < SB_HEAD_DIM
    zero = jnp.zeros((), BF16)
    row = lax.broadcasted_iota(jnp.int32, (2 * SB_BLOCK, 1), 0) % SB_BLOCK
    col = lax.broadcasted_iota(jnp.int32, (1, SB_BLOCK), 1)
    u = u_ref[...]

    def stacked_q(m):
        q = q_ref[0, pl.ds(pl.multiple_of(m * SB_BLOCK, SB_BLOCK), SB_BLOCK), :] * (SB_HEAD_DIM ** -0.5)
        return jnp.concatenate([jnp.where(low, q, zero), jnp.where(low, zero, q)], axis=0)

    def wide(m, near_start):
        qs = stacked_q(m)
        t_glob = m * SB_BLOCK + row
        kparts, vparts, masks = [], [], []
        for bi in range(SB_WIDE):
            jv = m - (SB_WIDE - 1) + bi
            k0 = pl.multiple_of(jnp.maximum(jv, 0) * SB_BLOCK, SB_BLOCK)
            kparts.append(k_ref[0, pl.ds(k0, SB_BLOCK), :])
            vparts.append(v_ref[0, pl.ds(k0, SB_BLOCK), :])
            s_glob = jv * SB_BLOCK + col
            causal = s_glob < t_glob if bi == SB_WIDE - 1 else None
            if near_start:
                real = jnp.broadcast_to(s_glob >= PAD, (2 * SB_BLOCK, SB_BLOCK))
                causal = real if causal is None else causal & real
            masks.append(causal)
        z = _dot_nt(qs, jnp.concatenate(kparts, axis=0))
        yield
        lsp, cs = [], []
        for bi in range(SB_WIDE):
            zb = z[:, bi * SB_BLOCK:(bi + 1) * SB_BLOCK]
            ls = _log_sigmoid(zb)
            l1m = ls - zb
            if masks[bi] is not None:
                l1m = jnp.where(masks[bi], l1m, 0.0)
            lsp.append(ls)
            cs.append(_sb_suffix(l1m, u))
        yield
        ws, later = [], None
        for bi in reversed(range(SB_WIDE)):
            tail = cs[bi][:, :SB_BLOCK]
            w = jnp.exp(lsp[bi] + (tail if later is None else tail + later))
            ws.append(w if masks[bi] is None else jnp.where(masks[bi], w, 0.0))
            tot = cs[bi][:, SB_BLOCK:]
            later = tot if later is None else later + tot
        w = jnp.concatenate(ws[::-1], axis=1)
        acc = _dot(w.astype(BF16), jnp.concatenate(vparts, axis=0))
        return later, acc

    def alive(c):
        return (jnp.max(c) >= SB_DEAD).astype(jnp.int32)

    def finish(m, car, acc):
        def cond(st):
            return (st[0] >= 0) & (st[1] > 0)

        def body(st):
            j, _, car, acc = st
            k0 = pl.multiple_of(j * SB_BLOCK, SB_BLOCK)
            mask = jnp.broadcast_to(k0 + col >= PAD, (2 * SB_BLOCK, SB_BLOCK))
            ls, l1m = _sb_scores(stacked_q(m), k_ref[0, pl.ds(k0, SB_BLOCK), :], mask)
            cs = _sb_suffix(l1m, u)
            w = jnp.where(mask, jnp.exp(ls + cs[:, :SB_BLOCK] + car), 0.0)
            acc = acc + _dot(w.astype(BF16), v_ref[0, pl.ds(k0, SB_BLOCK), :])
            car = car + cs[:, SB_BLOCK:]
            return j - 1, alive(car), car, acc

        _, _, _, acc = lax.while_loop(cond, body, (m - SB_WIDE, alive(car), car, acc))
        put(m, acc)

    def put(m, acc):
        out = jnp.where(low, acc[:SB_BLOCK], acc[SB_BLOCK:])
        o_ref[0, pl.ds(pl.multiple_of(m * SB_BLOCK, SB_BLOCK), SB_BLOCK), :] = out.astype(o_ref.dtype)

    def group(ms, near_start):
        if not ms:
            return
        states = _run_interleaved([wide(m, near_start) for m in ms])
        for m, (_, acc) in zip(ms, states):
            put(m, acc)
        worst = functools.reduce(jnp.maximum, [car for car, _ in states])
        more = (jnp.max(worst) >= SB_DEAD) & (ms[-1] >= SB_WIDE)

        @pl.when(more)
        def _():
            for m, (car, acc) in zip(ms, states):
                finish(m, car, acc)

    n_start = -(-SB_WIDE // SB_GROUP)
    for gi in range(n_start):
        group([jnp.int32(m) for m in range(gi * SB_GROUP, (gi + 1) * SB_GROUP) if m < nq], True)

    def steady(gi, carry):
        group([gi * SB_GROUP + r for r in range(SB_GROUP)], False)
        return carry

    lax.fori_loop(n_start, nq // SB_GROUP, steady, 0)
    group([jnp.int32(m) for m in range(max(nq // SB_GROUP, n_start) * SB_GROUP, nq)], False)


def _sb_attention(qkv3, u2):
    bsz, lp, _ = qkv3.shape
    n_pairs = D_MODEL // LANES
    assert lp >= SB_WIDE * SB_BLOCK

    def col_spec(offset):
        return pl.BlockSpec((1, lp, LANES), lambda b, p: (b, 0, offset + p))

    return pl.pallas_call(
        functools.partial(_sb_body, nq=lp // SB_BLOCK),
        grid=(bsz, n_pairs),
        in_specs=[col_spec(0), col_spec(n_pairs), col_spec(2 * n_pairs),
                  _const_spec((2 * SB_BLOCK, 2 * SB_BLOCK))],
        out_specs=pl.BlockSpec((1, lp, LANES), lambda b, p: (b, 0, p)),
        out_shape=jax.ShapeDtypeStruct((bsz, lp, D_MODEL), BF16),
        compiler_params=_params(2),
        name="sb_attention",
    )(qkv3, qkv3, qkv3, u2)


def _suffix_sum_matrix():
    j = jnp.arange(SB_BLOCK)[:, None]
    s = jnp.arange(2 * SB_BLOCK)[None, :]
    u = ((j > s) | (s >= SB_BLOCK)).astype(BF16)
    return jnp.concatenate([u, u], axis=0)


def kernel(x, meta_tokens, norm_gains, ffn_w_in, ffn_w_out, conv_w_in, conv_w, conv_w_out,
           gla_w_in, gla_w_gate_up, gla_b_gate, gla_norm, gla_w_out, sb_w_in, sb_w_out):
    bsz, seq, _ = x.shape
    depth = norm_gains.shape[0]
    lp = PAD + N_META + seq
    assert lp % SB_BLOCK == 0 and lp % GLA_CHUNK == 0
    t = bsz * lp
    tm = _row_tile(t, (512, 384, 256, 128))
    tm_ffn = _row_tile(t, (768, 512, 384, 256, 128))
    tb = _row_tile(lp, (384, 256, 128, 64))

    seq_head = jnp.concatenate([jnp.zeros((PAD, D_MODEL), x.dtype), meta_tokens.astype(x.dtype)], axis=0)
    pps = lp // SB_BLOCK
    tm_out = _row_tile(bsz * seq, (512, 256, 128))

    gains = norm_gains.reshape(depth, -1, 1, D_MODEL)
    ffn_w_in, ffn_w_out = ffn_w_in.astype(BF16), ffn_w_out.astype(BF16)
    tri = (jnp.arange(GLA_CHUNK)[:, None] >= jnp.arange(GLA_CHUNK)[None, :]).astype(BF16)
    u2 = _suffix_sum_matrix()

    h = x.reshape(bsz * seq, D_MODEL)
    for i in range(depth):
        g = gains[i]
        if i == 0:
            h = _ffn(h, g[0], g[1], ffn_w_in, ffn_w_out, i, 0, tm=tm_ffn, mode="enter", seq_head=seq_head,
                     pieces_per_seq=pps)
        else:
            h = _ffn(h, g[0], g[1], ffn_w_in, ffn_w_out, i, 0, tm=tm_ffn)
        kind, j = i % 3, i // 3
        if kind == 0:
            cw = jnp.pad(conv_w[j], ((0, 8 - CONV_WIDTH), (0, 0)))
            h = _conv_mixer(h, g[2], g[3], cw, conv_w_in[j].astype(BF16), conv_w_out[j].astype(BF16),
                            tm=tm_ffn, lp=lp)
        elif kind == 1:
            n_main = 2 * GLA_DK + 2 * GLA_DV
            w_a = jnp.pad(gla_w_in[j][:, n_main:], ((0, 0), (0, LANES - GLA_RANK))).astype(BF16)
            w_gu = jnp.pad(gla_w_gate_up[j], ((0, LANES - GLA_RANK), (0, 0))).astype(BF16)
            h = _gla_mixer(h.reshape(bsz, lp, D_MODEL), g[2], g[3], gla_w_in[j][:, :n_main].astype(BF16),
                           w_a, w_gu, gla_b_gate[j][None], gla_norm[j][None], gla_w_out[j].astype(BF16),
                           tri, tb=tb).reshape(t, D_MODEL)
        else:
            qkv = _proj_in(h, g[2], sb_w_in[j].astype(BF16), tm=tm_ffn)
            o = _sb_attention(qkv.reshape(bsz, lp, 3 * D_MODEL), u2)
            mix = (o.reshape(t, D_MODEL), g[3], sb_w_out[j].astype(BF16))
            if i == depth - 1:
                h = _proj_out(h, *mix, tm=tm)
        if i == depth - 1:
            h = _ffn(h, g[4], g[5], ffn_w_in, ffn_w_out, i, 1, tm=tm_out, mode="leave", pieces_per_seq=pps)
        else:
            h = _ffn(h, g[4], g[5], ffn_w_in, ffn_w_out, i, 1, tm=tm_ffn, mix=mix if kind == 2 else None)
    return h.reshape(bsz, seq, D_MODEL)
```

```python
import functools

import jax
import jax.numpy as jnp
from jax import lax
from jax.experimental import pallas as pl
from jax.experimental.pallas import tpu as pltpu

F32 = jnp.float32
BF16 = jnp.bfloat16

D_MODEL = 1024
D_FF = 2816
N_META = 16
RMS_EPS = 1e-6
CONV_WIDTH = 3
GLA_HEADS = 4
GLA_HK = 128
GLA_HV = 256
GLA_DK = GLA_HEADS * GLA_HK
GLA_DV = GLA_HEADS * GLA_HV
GLA_RANK = 16
GLA_GATE_NORMALIZER = 16.0
GLA_CHUNK = 64
GLA_SUB = 16
GLA_MILD_DECAY = 40.0
SB_HEAD_DIM = 64
SB_BLOCK = 128
LANES = 128
SUBLANES = 8
SUBLANES_BF16 = 16
PAD = SB_BLOCK - N_META

VMEM_LIMIT_BYTES = 56 * 1024 * 1024


def _row_tile(total, candidates=(1024, 512, 384, 256, 128)):
    for c in candidates:
        if total % c == 0:
            return c
    raise ValueError(f"no row tile for {total}")


def _rms(x, g):
    return x * lax.rsqrt(jnp.mean(x * x, axis=-1, keepdims=True) + RMS_EPS) * g


def _dot(a, b):
    return jnp.dot(a, b, preferred_element_type=F32)


def _dot_nt(a, b):
    return lax.dot_general(a, b, (((1,), (1,)), ((), ())), preferred_element_type=F32)


def _dot_tn(a, b):
    return lax.dot_general(a, b, (((0,), (0,)), ((), ())), preferred_element_type=F32)


def _sigmoid(x):
    return 1.0 / (1.0 + jnp.exp(-x))


def _log_sigmoid(x):
    return jnp.minimum(x, 0.0) - jnp.log(1.0 + jnp.exp(-jnp.abs(x)))


def _run_interleaved(chains):
    results = [None] * len(chains)
    live = list(range(len(chains)))
    while live:
        for idx in list(live):
            try:
                next(chains[idx])
            except StopIteration as done:
                results[idx] = done.value
                live.remove(idx)
    return results


def _const_spec(shape):
    return pl.BlockSpec(shape, lambda *_: (0,) * len(shape), pipeline_mode=pl.Buffered(1))


def _params(n_axes):
    return pltpu.CompilerParams(dimension_semantics=("arbitrary",) * n_axes,
                                vmem_limit_bytes=VMEM_LIMIT_BYTES)


def _ffn_body(*refs, n_pieces, pieces_per_seq, with_mix):
    pieces, rest = refs[:n_pieces], refs[n_pieces:]
    head_ref = None
    if pieces_per_seq is not None:
        head_ref, rest = rest[0], rest[1:]
    mix_refs = None
    if with_mix:
        mix_refs, rest = rest[:3], rest[3:]
    g0_ref, g1_ref, win_ref, wout_ref, o_ref = rest
    tm = o_ref.shape[0]
    piece_rows = tm // n_pieces

    def rows_of(lo, hi):
        if n_pieces == 1:
            return pieces[0][lo:hi, :]
        parts = []
        for r in range(lo // piece_rows, hi // piece_rows):
            part = pieces[r][...]
            if head_ref is not None:
                is_head = (pl.program_id(0) * n_pieces + r) % pieces_per_seq == 0
                part = jnp.where(is_head, head_ref[...], part)
            parts.append(part)
        return parts[0] if len(parts) == 1 else jnp.concatenate(parts, axis=0)

    def chain(lo, hi):
        h = rows_of(lo, hi)
        if mix_refs is not None:
            m_ref, gm_ref, wm_ref = mix_refs
            h = h + _rms(_dot(m_ref[lo:hi, :], wm_ref[...]), gm_ref[...])
        xb = _rms(h, g0_ref[...]).astype(BF16)
        yield
        gate = _dot(xb, win_ref[:, :D_FF])
        up = _dot(xb, win_ref[:, D_FF:])
        a = (gate * _sigmoid(gate) * up).astype(BF16)
        yield
        y = _dot(a, wout_ref[...])
        yield
        o_ref[lo:hi, :] = h + 0.5 * _rms(y, g1_ref[...])

    align = SUBLANES_BF16 if n_pieces == 1 else piece_rows
    n_split = next(c for c in (3, 2, 1) if tm % (c * align) == 0)
    step = tm // n_split
    _run_interleaved([chain(c * step, (c + 1) * step) for c in range(n_split)])


def _ffn(src, g_pre, g_post, w_in, w_out, layer, which, *, tm, mode="stream", seq_head=None, pieces_per_seq=None,
         mix=None):
    weights = [pl.BlockSpec((None, None, D_MODEL, 2 * D_FF), lambda i: (layer, which, 0, 0),
                            pipeline_mode=pl.Buffered(1)),
               pl.BlockSpec((None, None, D_FF, D_MODEL), lambda i: (layer, which, 0, 0),
                            pipeline_mode=pl.Buffered(1))]
    gains = [_const_spec((1, D_MODEL)), _const_spec((1, D_MODEL))]
    if mode == "stream":
        n, out_rows = 1, src.shape[0]
        pieces, extra, extra_specs = [pl.BlockSpec((tm, D_MODEL), lambda i: (i, 0))], [], []
        if mix is not None:
            extra = list(mix)
            extra_specs = [pl.BlockSpec((tm, mix[0].shape[1]), lambda i: (i, 0)), _const_spec((1, D_MODEL)),
                           _const_spec(mix[2].shape)]
    else:
        n = tm // SB_BLOCK
        x_per_seq = pieces_per_seq - 1
        if mode == "enter":
            index = lambda i, r: jnp.maximum(i * n + r - (i * n + r) // pieces_per_seq - 1, 0)
            out_rows = src.shape[0] // x_per_seq * pieces_per_seq
            extra, extra_specs = [seq_head], [_const_spec((SB_BLOCK, D_MODEL))]
        else:
            index = lambda i, r: i * n + r + (i * n + r) // x_per_seq + 1
            out_rows = src.shape[0] // pieces_per_seq * x_per_seq
            extra, extra_specs = [], []
        pieces = [pl.BlockSpec((SB_BLOCK, D_MODEL), functools.partial(lambda i, r: (index(i, r), 0), r=r))
                  for r in range(n)]
    return pl.pallas_call(
        functools.partial(_ffn_body, n_pieces=n, pieces_per_seq=pieces_per_seq if mode == "enter" else None,
                          with_mix=mix is not None),
        grid=(out_rows // tm,),
        in_specs=pieces + extra_specs + gains + weights,
        out_specs=pl.BlockSpec((tm, D_MODEL), lambda i: (i, 0)),
        out_shape=jax.ShapeDtypeStruct((out_rows, D_MODEL), F32),
        compiler_params=_params(1),
        name="ffn",
    )(*([src] * n), *extra, g_pre, g_post, w_in, w_out)


def _conv_body(h_ref, g2_ref, g3_ref, cw_ref, win_ref, wout_ref, o_ref, tail_ref, *, tm, lp, n_split):
    i = pl.program_id(0)
    g2 = g2_ref[...]
    cw = cw_ref[...]
    rows = tm // n_split
    row = lax.broadcasted_iota(jnp.int32, (rows, 1), 0)

    @pl.when(i == 0)
    def _():
        tail_ref[...] = jnp.zeros_like(tail_ref)

    before = [tail_ref[...]]

    def chain(c):
        lo = c * rows
        h = h_ref[lo:lo + rows, :]
        xb = _rms(h, g2).astype(BF16)
        yield
        pc = _dot(xb, win_ref[:, D_MODEL:2 * D_MODEL])
        ph = _dot(xb, win_ref[:, 2 * D_MODEL:])
        uu = jnp.where((i * tm + lo + row) % lp >= PAD, pc * ph, 0.0)
        before.append(uu[rows - SUBLANES:])
        yield
        tail = before[c]
        prev1 = jnp.where(row == 0, tail[SUBLANES - 1:], pltpu.roll(uu, 1, 0))
        prev2 = jnp.where(row == 0, tail[SUBLANES - 2:SUBLANES - 1],
                          jnp.where(row == 1, tail[SUBLANES - 1:], pltpu.roll(uu, 2, 0)))
        conv = cw[2:3] * uu + cw[1:2] * prev1 + cw[0:1] * prev2
        pb = _dot(xb, win_ref[:, :D_MODEL])
        y = _dot((pb * conv).astype(BF16), wout_ref[...])
        yield
        o_ref[lo:lo + rows, :] = h + _rms(y, g3_ref[...])

    _run_interleaved([chain(c) for c in range(n_split)])
    tail_ref[...] = before[n_split]


def _conv_mixer(h, g_pre, g_post, conv_w, w_in, w_out, *, tm, lp):
    t = h.shape[0]
    row = pl.BlockSpec((tm, D_MODEL), lambda i: (i, 0))
    n_split = next(c for c in (3, 2, 1) if tm % (SUBLANES_BF16 * c) == 0)
    return pl.pallas_call(
        functools.partial(_conv_body, tm=tm, lp=lp, n_split=n_split),
        grid=(t // tm,),
        in_specs=[row, _const_spec((1, D_MODEL)), _const_spec((1, D_MODEL)),
                  _const_spec((SUBLANES, D_MODEL)), _const_spec((D_MODEL, 3 * D_MODEL)),
                  _const_spec((D_MODEL, D_MODEL))],
        out_specs=row,
        out_shape=jax.ShapeDtypeStruct(h.shape, F32),
        scratch_shapes=[pltpu.VMEM((SUBLANES, D_MODEL), F32)],
        compiler_params=_params(1),
        name="conv_mixer",
    )(h, g_pre, g_post, conv_w, w_in, w_out)


def _split3(x):
    hi = x.astype(BF16)
    r = x - hi.astype(F32)
    mid = r.astype(BF16)
    lo = (r - mid.astype(F32)).astype(BF16)
    return hi, mid, lo


def _gla_chunk_head(q, k, v, b, s_t):
    n_sub = GLA_CHUNK // GLA_SUB
    sub_row = lax.broadcasted_iota(jnp.int32, (GLA_SUB, 1), 0)
    col = lax.broadcasted_iota(jnp.int32, (1, GLA_CHUNK), 1)
    att_rows = []
    for i in range(n_sub):
        r0 = i * GLA_SUB
        qd, kd, bd = q[r0:r0 + GLA_SUB], k[r0:r0 + GLA_SUB], b[r0:r0 + GLA_SUB]
        blk = jnp.zeros((GLA_SUB, GLA_CHUNK), F32)
        if i > 0:
            b_ref = b[r0 - 1:r0]
            q_t = (qd * jnp.exp(bd - b_ref)).astype(BF16)
            k_t = (k * jnp.exp(jnp.minimum(b_ref - b, 0.0))).astype(BF16)
            blk = jnp.where(col < r0, _dot_nt(q_t, k_t), 0.0)
        for s in range(GLA_SUB):
            e = jnp.exp(jnp.where(sub_row >= s, bd - bd[s:s + 1], -jnp.inf))
            a_col = jnp.sum(qd * e * kd[s:s + 1], axis=-1, keepdims=True)
            blk = jnp.where(col == r0 + s, a_col, blk)
        att_rows.append(blk)
    att = jnp.concatenate(att_rows, axis=0)
    o = _dot(att.astype(BF16), v.astype(BF16))
    o = o + _dot_nt((q * jnp.exp(b)).astype(BF16), s_t.astype(BF16))
    b_last = b[GLA_CHUNK - 1:GLA_CHUNK]
    k_dec = (k * jnp.exp(b_last - b)).astype(BF16)
    s_new = s_t * jnp.exp(b_last) + _dot_tn(v.astype(BF16), k_dec)
    return o, s_new


def _gla_chunks_mild(chunks, states):
    t_row = lax.broadcasted_iota(jnp.int32, (GLA_CHUNK, 1), 0)
    s_col = lax.broadcasted_iota(jnp.int32, (1, GLA_CHUNK), 1)
    staged = []
    for heads in chunks:
        per_head = []
        for q, k, v, b in heads:
            q_dec = (q * jnp.exp(b)).astype(BF16)
            att = _dot_nt(q_dec, (k * jnp.exp(-b)).astype(BF16))
            b_last = b[GLA_CHUNK - 1:GLA_CHUNK]
            vb = v.astype(BF16)
            grow = _dot_tn(vb, (k * jnp.exp(b_last - b)).astype(BF16))
            per_head.append((q_dec, att, vb, jnp.exp(b_last), grow))
        staged.append(per_head)
    from_state = []
    for per_head in staged:
        from_state.append([_dot_nt(q_dec, s_t.astype(BF16)) for (q_dec, *_), s_t in zip(per_head, states)])
        states = [s_t * decay + grow for (_, _, _, decay, grow), s_t in zip(per_head, states)]
    outs = []
    for per_head, o_states in zip(staged, from_state):
        outs.append([_dot(jnp.where(s_col <= t_row, att, 0.0).astype(BF16), vb) + o_state
                     for (_, att, vb, _, _), o_state in zip(per_head, o_states)])
    return outs, states


def _gla_body(h_ref, g2_ref, g3_ref, win_ref, wa_ref, wgu_ref, bg_ref, gn_ref, wout_ref, tri_ref,
              o_ref, s_ref, p_ref, b_ref, oc_ref, mild_ref, *, tb):
    jb = pl.program_id(1)

    @pl.when(jb == 0)
    def _():
        s_ref[...] = jnp.zeros_like(s_ref)

    half = tb // 2 if tb % (2 * GLA_CHUNK) == 0 else tb
    tri = tri_ref[...]

    def project(lo):
        ub = _rms(h_ref[0, lo:lo + half, :], g2_ref[...]).astype(BF16)
        yield
        p = _dot(ub, win_ref[...])
        a = _dot(ub, wa_ref[...]).astype(BF16)
        pos = jb * tb + lo + lax.broadcasted_iota(jnp.int32, (half, 1), 0)
        valid = pos >= PAD
        rows = slice(lo, lo + half)
        p_ref[rows, :GLA_DK] = p[:, :GLA_DK] * (GLA_HK ** -0.5)
        p_ref[rows, GLA_DK:2 * GLA_DK] = jnp.where(valid, p[:, GLA_DK:2 * GLA_DK], 0.0)
        p_ref[rows, 2 * GLA_DK:2 * GLA_DK + GLA_DV] = jnp.where(valid, p[:, 2 * GLA_DK:2 * GLA_DK + GLA_DV], 0.0)
        p_ref[rows, 2 * GLA_DK + GLA_DV:] = p[:, 2 * GLA_DK + GLA_DV:]
        yield
        x = _dot(a, wgu_ref[...]) + bg_ref[...]
        la = _log_sigmoid(x) * (1.0 / GLA_GATE_NORMALIZER)
        for c in range(half // GLA_CHUNK):
            b_c = sum(_dot(tri, part) for part in _split3(la[c * GLA_CHUNK:(c + 1) * GLA_CHUNK]))
            b_ref[lo + c * GLA_CHUNK:lo + (c + 1) * GLA_CHUNK, :] = b_c
            mild_ref[lo // GLA_CHUNK + c] = (jnp.min(b_c[GLA_CHUNK - 1:]) > -GLA_MILD_DECAY).astype(jnp.int32)

    _run_interleaved([project(lo) for lo in range(0, tb, half)])

    n_chunks = tb // GLA_CHUNK
    per_iter = next(c for c in (3, 2, 1) if n_chunks % c == 0)

    def chunk_group(ci, carry):
        starts = [pl.multiple_of((ci * per_iter + r) * GLA_CHUNK, GLA_CHUNK) for r in range(per_iter)]
        mild = functools.reduce(jnp.logical_and, [mild_ref[ci * per_iter + r] > 0 for r in range(per_iter)])

        def load(r0):
            rows = pl.ds(r0, GLA_CHUNK)
            b_all = b_ref[rows, :]
            return [(p_ref[rows, hh * GLA_HK:(hh + 1) * GLA_HK],
                     p_ref[rows, GLA_DK + hh * GLA_HK:GLA_DK + (hh + 1) * GLA_HK],
                     p_ref[rows, 2 * GLA_DK + hh * GLA_HV:2 * GLA_DK + (hh + 1) * GLA_HV],
                     b_all[:, hh * GLA_HK:(hh + 1) * GLA_HK]) for hh in range(GLA_HEADS)]

        def store(r0, outs):
            for hh, o in enumerate(outs):
                oc_ref[pl.ds(r0, GLA_CHUNK), hh * GLA_HV:(hh + 1) * GLA_HV] = o

        @pl.when(mild)
        def _():
            outs, states = _gla_chunks_mild([load(r0) for r0 in starts], [s_ref[hh] for hh in range(GLA_HEADS)])
            for r0, chunk_outs in zip(starts, outs):
                store(r0, chunk_outs)
            for hh, s_new in enumerate(states):
                s_ref[hh] = s_new

        @pl.when(jnp.logical_not(mild))
        def _():
            for r0 in starts:
                outs = []
                for hh, head in enumerate(load(r0)):
                    o, s_new = _gla_chunk_head(*head, s_ref[hh])
                    s_ref[hh] = s_new
                    outs.append(o)
                store(r0, outs)

        return carry

    lax.fori_loop(0, n_chunks // per_iter, chunk_group, 0)

    gn = gn_ref[...]

    def finish(lo):
        rows = slice(lo, lo + half)
        normed = []
        for hh in range(GLA_HEADS):
            oh = oc_ref[rows, hh * GLA_HV:(hh + 1) * GLA_HV]
            normed.append(oh * lax.rsqrt(jnp.mean(oh * oh, axis=-1, keepdims=True) + RMS_EPS) * gn)
        gate = p_ref[rows, 2 * GLA_DK + GLA_DV:]
        y = (jnp.concatenate(normed, axis=-1) * (gate * _sigmoid(gate))).astype(BF16)
        yield
        z = _dot(y, wout_ref[...])
        yield
        o_ref[0, rows, :] = h_ref[0, rows, :] + _rms(z, g3_ref[...])

    _run_interleaved([finish(lo) for lo in range(0, tb, half)])


def _gla_mixer(h3, g_pre, g_post, w_in, w_a, w_gu, b_gate, g_norm, w_out, tri, *, tb):
    bsz, lp, _ = h3.shape
    blk = pl.BlockSpec((1, tb, D_MODEL), lambda b, j: (b, j, 0))
    n_in = 2 * GLA_DK + 2 * GLA_DV
    return pl.pallas_call(
        functools.partial(_gla_body, tb=tb),
        grid=(bsz, lp // tb),
        in_specs=[blk, _const_spec((1, D_MODEL)), _const_spec((1, D_MODEL)),
                  _const_spec((D_MODEL, n_in)), _const_spec((D_MODEL, LANES)),
                  _const_spec((LANES, GLA_DK)), _const_spec((1, GLA_DK)), _const_spec((1, GLA_HV)),
                  _const_spec((GLA_DV, D_MODEL)), _const_spec((GLA_CHUNK, GLA_CHUNK))],
        out_specs=blk,
        out_shape=jax.ShapeDtypeStruct(h3.shape, F32),
        scratch_shapes=[pltpu.VMEM((GLA_HEADS, GLA_HV, GLA_HK), F32),
                        pltpu.VMEM((tb, n_in), F32),
                        pltpu.VMEM((tb, GLA_DK), F32),
                        pltpu.VMEM((tb, GLA_DV), F32),
                        pltpu.SMEM((tb // GLA_CHUNK,), jnp.int32)],
        compiler_params=_params(2),
        name="gla_mixer",
    )(h3, g_pre, g_post, w_in, w_a, w_gu, b_gate, g_norm, w_out, tri)


def _proj_in_body(h_ref, g_ref, w_ref, o_ref):
    tm = h_ref.shape[0]
    n_split = next(c for c in (3, 2, 1) if tm % (SUBLANES_BF16 * c) == 0)
    rows = tm // n_split

    def chain(lo):
        xb = _rms(h_ref[lo:lo + rows, :], g_ref[...]).astype(BF16)
        yield
        o_ref[lo:lo + rows, :] = _dot(xb, w_ref[...]).astype(o_ref.dtype)

    _run_interleaved([chain(c * rows) for c in range(n_split)])


def _proj_in(h, g, w, *, tm):
    t, n = h.shape[0], w.shape[1]
    return pl.pallas_call(
        _proj_in_body,
        grid=(t // tm,),
        in_specs=[pl.BlockSpec((tm, D_MODEL), lambda i: (i, 0)), _const_spec((1, D_MODEL)),
                  _const_spec(w.shape)],
        out_specs=pl.BlockSpec((tm, n), lambda i: (i, 0)),
        out_shape=jax.ShapeDtypeStruct((t, n), BF16),
        compiler_params=_params(1),
        name="proj_in",
    )(h, g, w)


def _proj_out_body(h_ref, x_ref, g_ref, w_ref, o_ref):
    o_ref[...] = h_ref[...] + _rms(_dot(x_ref[...], w_ref[...]), g_ref[...])


def _proj_out(h, x, g, w, *, tm):
    t, kdim = x.shape
    row = pl.BlockSpec((tm, D_MODEL), lambda i: (i, 0))
    return pl.pallas_call(
        _proj_out_body,
        grid=(t // tm,),
        in_specs=[row, pl.BlockSpec((tm, kdim), lambda i: (i, 0)), _const_spec((1, D_MODEL)),
                  _const_spec(w.shape)],
        out_specs=row,
        out_shape=jax.ShapeDtypeStruct(h.shape, F32),
        compiler_params=_params(1),
        name="proj_out",
    )(h, x, g, w)


SB_WIDE = 3
SB_DEAD = -104.0
SB_GROUP = 3


def _sb_scores(qs, kwin, mask):
    z = _dot_nt(qs, kwin)
    ls = _log_sigmoid(z)
    return ls, jnp.where(mask, ls - z, 0.0)


def _sb_suffix(l1m, u):
    hi = l1m.astype(BF16)
    lo = (l1m - hi.astype(F32)).astype(BF16)
    return _dot(jnp.concatenate([hi, lo], axis=1), u)


def _sb_body(q_ref, k_ref, v_ref, u_ref, o_ref, *, nq):
    lane = lax.broadcasted_iota(jnp.int32, (1, LANES), 1)
    low = lane < SB_HEAD_DIM
    zero = jnp.zeros((), BF16)
    row = lax.broadcasted_iota(jnp.int32, (2 * SB_BLOCK, 1), 0) % SB_BLOCK
    col = lax.broadcasted_iota(jnp.int32, (1, SB_BLOCK), 1)
    u = u_ref[...]

    def stacked_q(m):
        q = q_ref[0, pl.ds(pl.multiple_of(m * SB_BLOCK, SB_BLOCK), SB_BLOCK), :] * (SB_HEAD_DIM ** -0.5)
        return jnp.concatenate([jnp.where(low, q, zero), jnp.where(low, zero, q)], axis=0)

    def wide(m, near_start):
        qs = stacked_q(m)
        t_glob = m * SB_BLOCK + row
        kparts, vparts, masks = [], [], []
        for bi in range(SB_WIDE):
            jv = m - (SB_WIDE - 1) + bi
            k0 = pl.multiple_of(jnp.maximum(jv, 0) * SB_BLOCK, SB_BLOCK)
            kparts.append(k_ref[0, pl.ds(k0, SB_BLOCK), :])
            vparts.append(v_ref[0, pl.ds(k0, SB_BLOCK), :])
            s_glob = jv * SB_BLOCK + col
            causal = s_glob < t_glob if bi == SB_WIDE - 1 else None
            if near_start:
                real = jnp.broadcast_to(s_glob >= PAD, (2 * SB_BLOCK, SB_BLOCK))
                causal = real if causal is None else causal & real
            masks.append(causal)
        z = _dot_nt(qs, jnp.concatenate(kparts, axis=0))
        yield
        lsp, cs = [], []
        for bi in range(SB_WIDE):
            zb = z[:, bi * SB_BLOCK:(bi + 1) * SB_BLOCK]
            ls = _log_sigmoid(zb)
            l1m = ls - zb
            if masks[bi] is not None:
                l1m = jnp.where(masks[bi], l1m, 0.0)
            lsp.append(ls)
            cs.append(_sb_suffix(l1m, u))
        yield
        ws, later = [], None
        for bi in reversed(range(SB_WIDE)):
            tail = cs[bi][:, :SB_BLOCK]
            w = jnp.exp(lsp[bi] + (tail if later is None else tail + later))
            ws.append(w if masks[bi] is None else jnp.where(masks[bi], w, 0.0))
            tot = cs[bi][:, SB_BLOCK:]
            later = tot if later is None else later + tot
        w = jnp.concatenate(ws[::-1], axis=1)
        acc = _dot(w.astype(BF16), jnp.concatenate(vparts, axis=0))
        return later, acc

    def alive(c):
        return (jnp.max(c) >= SB_DEAD).astype(jnp.int32)

    def finish(m, car, acc):
        def cond(st):
            return (st[0] >= 0) & (st[1] > 0)

        def body(st):
            j, _, car, acc = st
            k0 = pl.multiple_of(j * SB_BLOCK, SB_BLOCK)
            mask = jnp.broadcast_to(k0 + col >= PAD, (2 * SB_BLOCK, SB_BLOCK))
            ls, l1m = _sb_scores(stacked_q(m), k_ref[0, pl.ds(k0, SB_BLOCK), :], mask)
            cs = _sb_suffix(l1m, u)
            w = jnp.where(mask, jnp.exp(ls + cs[:, :SB_BLOCK] + car), 0.0)
            acc = acc + _dot(w.astype(BF16), v_ref[0, pl.ds(k0, SB_BLOCK), :])
            car = car + cs[:, SB_BLOCK:]
            return j - 1, alive(car), car, acc

        _, _, _, acc = lax.while_loop(cond, body, (m - SB_WIDE, alive(car), car, acc))
        put(m, acc)

    def put(m, acc):
        out = jnp.where(low, acc[:SB_BLOCK], acc[SB_BLOCK:])
        o_ref[0, pl.ds(pl.multiple_of(m * SB_BLOCK, SB_BLOCK), SB_BLOCK), :] = out.astype(o_ref.dtype)

    def group(ms, near_start):
        if not ms:
            return
        states = _run_interleaved([wide(m, near_start) for m in ms])
        for m, (_, acc) in zip(ms, states):
            put(m, acc)
        worst = functools.reduce(jnp.maximum, [car for car, _ in states])
        more = (jnp.max(worst) >= SB_DEAD) & (ms[-1] >= SB_WIDE)

        @pl.when(more)
        def _():
            for m, (car, acc) in zip(ms, states):
                finish(m, car, acc)

    n_start = -(-SB_WIDE // SB_GROUP)
    for gi in range(n_start):
        group([jnp.int32(m) for m in range(gi * SB_GROUP, (gi + 1) * SB_GROUP) if m < nq], True)

    def steady(gi, carry):
        group([gi * SB_GROUP + r for r in range(SB_GROUP)], False)
        return carry

    lax.fori_loop(n_start, nq // SB_GROUP, steady, 0)
    group([jnp.int32(m) for m in range(max(nq // SB_GROUP, n_start) * SB_GROUP, nq)], False)


def _sb_attention(qkv3, u2):
    bsz, lp, _ = qkv3.shape
    n_pairs = D_MODEL // LANES
    assert lp >= SB_WIDE * SB_BLOCK

    def col_spec(offset):
        return pl.BlockSpec((1, lp, LANES), lambda b, p: (b, 0, offset + p))

    return pl.pallas_call(
        functools.partial(_sb_body, nq=lp // SB_BLOCK),
        grid=(bsz, n_pairs),
        in_specs=[col_spec(0), col_spec(n_pairs), col_spec(2 * n_pairs),
                  _const_spec((2 * SB_BLOCK, 2 * SB_BLOCK))],
        out_specs=pl.BlockSpec((1, lp, LANES), lambda b, p: (b, 0, p)),
        out_shape=jax.ShapeDtypeStruct((bsz, lp, D_MODEL), BF16),
        compiler_params=_params(2),
        name="sb_attention",
    )(qkv3, qkv3, qkv3, u2)


def _suffix_sum_matrix():
    j = jnp.arange(SB_BLOCK)[:, None]
    s = jnp.arange(2 * SB_BLOCK)[None, :]
    u = ((j > s) | (s >= SB_BLOCK)).astype(BF16)
    return jnp.concatenate([u, u], axis=0)


def kernel(x, meta_tokens, norm_gains, ffn_w_in, ffn_w_out, conv_w_in, conv_w, conv_w_out,
           gla_w_in, gla_w_gate_up, gla_b_gate, gla_norm, gla_w_out, sb_w_in, sb_w_out):
    bsz, seq, _ = x.shape
    depth = norm_gains.shape[0]
    lp = PAD + N_META + seq
    assert lp % SB_BLOCK == 0 and lp % GLA_CHUNK == 0
    t = bsz * lp
    tm = _row_tile(t, (512, 384, 256, 128))
    tm_ffn = _row_tile(t, (768, 512, 384, 256, 128))
    tb = _row_tile(lp, (384, 256, 128, 64))

    seq_head = jnp.concatenate([jnp.zeros((PAD, D_MODEL), x.dtype), meta_tokens.astype(x.dtype)], axis=0)
    pps = lp // SB_BLOCK
    tm_out = _row_tile(bsz * seq, (512, 256, 128))

    gains = norm_gains.reshape(depth, -1, 1, D_MODEL)
    ffn_w_in, ffn_w_out = ffn_w_in.astype(BF16), ffn_w_out.astype(BF16)
    tri = (jnp.arange(GLA_CHUNK)[:, None] >= jnp.arange(GLA_CHUNK)[None, :]).astype(BF16)
    u2 = _suffix_sum_matrix()

    h = x.reshape(bsz * seq, D_MODEL)
    for i in range(depth):
        g = gains[i]
        if i == 0:
            h = _ffn(h, g[0], g[1], ffn_w_in, ffn_w_out, i, 0, tm=tm_ffn, mode="enter", seq_head=seq_head,
                     pieces_per_seq=pps)
        else:
            h = _ffn(h, g[0], g[1], ffn_w_in, ffn_w_out, i, 0, tm=tm_ffn)
        kind, j = i % 3, i // 3
        if kind == 0:
            cw = jnp.pad(conv_w[j], ((0, SUBLANES - CONV_WIDTH), (0, 0)))
            h = _conv_mixer(h, g[2], g[3], cw, conv_w_in[j].astype(BF16), conv_w_out[j].astype(BF16),
                            tm=tm_ffn, lp=lp)
        elif kind == 1:
            n_main = 2 * GLA_DK + 2 * GLA_DV
            w_a = jnp.pad(gla_w_in[j][:, n_main:], ((0, 0), (0, LANES - GLA_RANK))).astype(BF16)
            w_gu = jnp.pad(gla_w_gate_up[j], ((0, LANES - GLA_RANK), (0, 0))).astype(BF16)
            h = _gla_mixer(h.reshape(bsz, lp, D_MODEL), g[2], g[3], gla_w_in[j][:, :n_main].astype(BF16),
                           w_a, w_gu, gla_b_gate[j][None], gla_norm[j][None], gla_w_out[j].astype(BF16),
                           tri, tb=tb).reshape(t, D_MODEL)
        else:
            qkv = _proj_in(h, g[2], sb_w_in[j].astype(BF16), tm=tm_ffn)
            o = _sb_attention(qkv.reshape(bsz, lp, 3 * D_MODEL), u2)
            mix = (o.reshape(t, D_MODEL), g[3], sb_w_out[j].astype(BF16))
            if i == depth - 1:
                h = _proj_out(h, *mix, tm=tm)
        if i == depth - 1:
            h = _ffn(h, g[4], g[5], ffn_w_in, ffn_w_out, i, 1, tm=tm_out, mode="leave", pieces_per_seq=pps)
        else:
            h = _ffn(h, g[4], g[5], ffn_w_in, ffn_w_out, i, 1, tm=tm_ffn, mix=mix if kind == 2 else None)
    return h.reshape(bsz, seq, D_MODEL)
```

```python
import functools

import jax
import jax.numpy as jnp
from jax import lax
from jax.experimental import pallas as pl
from jax.experimental.pallas import tpu as pltpu

F32 = jnp.float32
BF16 = jnp.bfloat16

D_MODEL = 1024
D_FF = 2816
N_META = 16
RMS_EPS = 1e-6
CONV_WIDTH = 3
GLA_HEADS = 4
GLA_HK = 128
GLA_HV = 256
GLA_DK = GLA_HEADS * GLA_HK
GLA_DV = GLA_HEADS * GLA_HV
GLA_RANK = 16
GLA_GATE_NORMALIZER = 16.0
GLA_CHUNK = 64
GLA_SUB = 16
GLA_MILD_DECAY = 40.0
SB_HEAD_DIM = 64
SB_BLOCK = 128
LANES = 128
SUBLANES = 8
SUBLANES_BF16 = 16
PAD = SB_BLOCK - N_META

VMEM_LIMIT_BYTES = 56 * 1024 * 1024


def _row_tile(total, candidates=(1024, 512, 384, 256, 128)):
    for c in candidates:
        if total % c == 0:
            return c
    raise ValueError(f"no row tile for {total}")


def _rms(x, g):
    return x * lax.rsqrt(jnp.mean(x * x, axis=-1, keepdims=True) + RMS_EPS) * g


def _dot(a, b):
    return jnp.dot(a, b, preferred_element_type=F32)


def _dot_nt(a, b):
    return lax.dot_general(a, b, (((1,), (1,)), ((), ())), preferred_element_type=F32)


def _dot_tn(a, b):
    return lax.dot_general(a, b, (((0,), (0,)), ((), ())), preferred_element_type=F32)


def _sigmoid(x):
    return 1.0 / (1.0 + jnp.exp(-x))


def _log_sigmoid(x):
    return jnp.minimum(x, 0.0) - jnp.log(1.0 + jnp.exp(-jnp.abs(x)))


def _run_interleaved(chains):
    results = [None] * len(chains)
    live = list(range(len(chains)))
    while live:
        for idx in list(live):
            try:
                next(chains[idx])
            except StopIteration as done:
                results[idx] = done.value
                live.remove(idx)
    return results


def _const_spec(shape):
    return pl.BlockSpec(shape, lambda *_: (0,) * len(shape), pipeline_mode=pl.Buffered(1))


def _params(n_axes):
    return pltpu.CompilerParams(dimension_semantics=("arbitrary",) * n_axes,
                                vmem_limit_bytes=VMEM_LIMIT_BYTES)


def _ffn_body(*refs, n_pieces, pieces_per_seq, with_mix):
    pieces, rest = refs[:n_pieces], refs[n_pieces:]
    head_ref = None
    if pieces_per_seq is not None:
        head_ref, rest = rest[0], rest[1:]
    mix_refs = None
    if with_mix:
        mix_refs, rest = rest[:3], rest[3:]
    g0_ref, g1_ref, win_ref, wout_ref, o_ref = rest
    tm = o_ref.shape[0]
    piece_rows = tm // n_pieces

    def rows_of(lo, hi):
        if n_pieces == 1:
            return pieces[0][lo:hi, :]
        parts = []
        for r in range(lo // piece_rows, hi // piece_rows):
            part = pieces[r][...]
            if head_ref is not None:
                is_head = (pl.program_id(0) * n_pieces + r) % pieces_per_seq == 0
                part = jnp.where(is_head, head_ref[...], part)
            parts.append(part)
        return parts[0] if len(parts) == 1 else jnp.concatenate(parts, axis=0)

    def chain(lo, hi):
        h = rows_of(lo, hi)
        if mix_refs is not None:
            m_ref, gm_ref, wm_ref = mix_refs
            h = h + _rms(_dot(m_ref[lo:hi, :], wm_ref[...]), gm_ref[...])
        xb = _rms(h, g0_ref[...]).astype(BF16)
        yield
        gate = _dot(xb, win_ref[:, :D_FF])
        up = _dot(xb, win_ref[:, D_FF:])
        a = (gate * _sigmoid(gate) * up).astype(BF16)
        yield
        y = _dot(a, wout_ref[...])
        yield
        o_ref[lo:hi, :] = h + 0.5 * _rms(y, g1_ref[...])

    align = SUBLANES_BF16 if n_pieces == 1 else piece_rows
    n_split = next(c for c in (3, 2, 1) if tm % (c * align) == 0)
    step = tm // n_split
    _run_interleaved([chain(c * step, (c + 1) * step) for c in range(n_split)])


def _ffn(src, g_pre, g_post, w_in, w_out, layer, which, *, tm, mode="stream", seq_head=None, pieces_per_seq=None,
         mix=None):
    weights = [pl.BlockSpec((None, None, D_MODEL, 2 * D_FF), lambda i: (layer, which, 0, 0),
                            pipeline_mode=pl.Buffered(1)),
               pl.BlockSpec((None, None, D_FF, D_MODEL), lambda i: (layer, which, 0, 0),
                            pipeline_mode=pl.Buffered(1))]
    gains = [_const_spec((1, D_MODEL)), _const_spec((1, D_MODEL))]
    if mode == "stream":
        n, out_rows = 1, src.shape[0]
        pieces, extra, extra_specs = [pl.BlockSpec((tm, D_MODEL), lambda i: (i, 0))], [], []
        if mix is not None:
            extra = list(mix)
            extra_specs = [pl.BlockSpec((tm, mix[0].shape[1]), lambda i: (i, 0)), _const_spec((1, D_MODEL)),
                           _const_spec(mix[2].shape)]
    else:
        n = tm // SB_BLOCK
        x_per_seq = pieces_per_seq - 1
        if mode == "enter":
            index = lambda i, r: jnp.maximum(i * n + r - (i * n + r) // pieces_per_seq - 1, 0)
            out_rows = src.shape[0] // x_per_seq * pieces_per_seq
            extra, extra_specs = [seq_head], [_const_spec((SB_BLOCK, D_MODEL))]
        else:
            index = lambda i, r: i * n + r + (i * n + r) // x_per_seq + 1
            out_rows = src.shape[0] // pieces_per_seq * x_per_seq
            extra, extra_specs = [], []
        pieces = [pl.BlockSpec((SB_BLOCK, D_MODEL), functools.partial(lambda i, r: (index(i, r), 0), r=r))
                  for r in range(n)]
    return pl.pallas_call(
        functools.partial(_ffn_body, n_pieces=n, pieces_per_seq=pieces_per_seq if mode == "enter" else None,
                          with_mix=mix is not None),
        grid=(out_rows // tm,),
        in_specs=pieces + extra_specs + gains + weights,
        out_specs=pl.BlockSpec((tm, D_MODEL), lambda i: (i, 0)),
        out_shape=jax.ShapeDtypeStruct((out_rows, D_MODEL), F32),
        compiler_params=_params(1),
        name="ffn",
    )(*([src] * n), *extra, g_pre, g_post, w_in, w_out)


def _conv_body(h_ref, g2_ref, g3_ref, cw_ref, win_ref, wout_ref, o_ref, tail_ref, *, tm, lp, n_split):
    i = pl.program_id(0)
    g2 = g2_ref[...]
    cw = cw_ref[...]
    rows = tm // n_split
    row = lax.broadcasted_iota(jnp.int32, (rows, 1), 0)

    @pl.when(i == 0)
    def _():
        tail_ref[...] = jnp.zeros_like(tail_ref)

    before = [tail_ref[...]]

    def chain(c):
        lo = c * rows
        h = h_ref[lo:lo + rows, :]
        xb = _rms(h, g2).astype(BF16)
        yield
        pc = _dot(xb, win_ref[:, D_MODEL:2 * D_MODEL])
        ph = _dot(xb, win_ref[:, 2 * D_MODEL:])
        uu = jnp.where((i * tm + lo + row) % lp >= PAD, pc * ph, 0.0)
        before.append(uu[rows - SUBLANES:])
        yield
        tail = before[c]
        prev1 = jnp.where(row == 0, tail[SUBLANES - 1:], pltpu.roll(uu, 1, 0))
        prev2 = jnp.where(row == 0, tail[SUBLANES - 2:SUBLANES - 1],
                          jnp.where(row == 1, tail[SUBLANES - 1:], pltpu.roll(uu, 2, 0)))
        conv = cw[2:3] * uu + cw[1:2] * prev1 + cw[0:1] * prev2
        pb = _dot(xb, win_ref[:, :D_MODEL])
        y = _dot((pb * conv).astype(BF16), wout_ref[...])
        yield
        o_ref[lo:lo + rows, :] = h + _rms(y, g3_ref[...])

    _run_interleaved([chain(c) for c in range(n_split)])
    tail_ref[...] = before[n_split]


def _conv_mixer(h, g_pre, g_post, conv_w, w_in, w_out, *, tm, lp):
    t = h.shape[0]
    row = pl.BlockSpec((tm, D_MODEL), lambda i: (i, 0))
    n_split = next(c for c in (3, 2, 1) if tm % (SUBLANES_BF16 * c) == 0)
    return pl.pallas_call(
        functools.partial(_conv_body, tm=tm, lp=lp, n_split=n_split),
        grid=(t // tm,),
        in_specs=[row, _const_spec((1, D_MODEL)), _const_spec((1, D_MODEL)),
                  _const_spec((SUBLANES, D_MODEL)), _const_spec((D_MODEL, 3 * D_MODEL)),
                  _const_spec((D_MODEL, D_MODEL))],
        out_specs=row,
        out_shape=jax.ShapeDtypeStruct(h.shape, F32),
        scratch_shapes=[pltpu.VMEM((SUBLANES, D_MODEL), F32)],
        compiler_params=_params(1),
        name="conv_mixer",
    )(h, g_pre, g_post, conv_w, w_in, w_out)


def _split3(x):
    hi = x.astype(BF16)
    r = x - hi.astype(F32)
    mid = r.astype(BF16)
    lo = (r - mid.astype(F32)).astype(BF16)
    return hi, mid, lo


def _gla_chunk_head(q, k, v, b, s_t):
    n_sub = GLA_CHUNK // GLA_SUB
    sub_row = lax.broadcasted_iota(jnp.int32, (GLA_SUB, 1), 0)
    col = lax.broadcasted_iota(jnp.int32, (1, GLA_CHUNK), 1)
    att_rows = []
    for i in range(n_sub):
        r0 = i * GLA_SUB
        qd, kd, bd = q[r0:r0 + GLA_SUB], k[r0:r0 + GLA_SUB], b[r0:r0 + GLA_SUB]
        blk = jnp.zeros((GLA_SUB, GLA_CHUNK), F32)
        if i > 0:
            b_ref = b[r0 - 1:r0]
            q_t = (qd * jnp.exp(bd - b_ref)).astype(BF16)
            k_t = (k * jnp.exp(jnp.minimum(b_ref - b, 0.0))).astype(BF16)
            blk = jnp.where(col < r0, _dot_nt(q_t, k_t), 0.0)
        for s in range(GLA_SUB):
            e = jnp.exp(jnp.where(sub_row >= s, bd - bd[s:s + 1], -jnp.inf))
            a_col = jnp.sum(qd * e * kd[s:s + 1], axis=-1, keepdims=True)
            blk = jnp.where(col == r0 + s, a_col, blk)
        att_rows.append(blk)
    att = jnp.concatenate(att_rows, axis=0)
    o = _dot(att.astype(BF16), v.astype(BF16))
    o = o + _dot_nt((q * jnp.exp(b)).astype(BF16), s_t.astype(BF16))
    b_last = b[GLA_CHUNK - 1:GLA_CHUNK]
    k_dec = (k * jnp.exp(b_last - b)).astype(BF16)
    s_new = s_t * jnp.exp(b_last) + _dot_tn(v.astype(BF16), k_dec)
    return o, s_new


def _gla_chunks_mild(chunks, states):
    t_row = lax.broadcasted_iota(jnp.int32, (GLA_CHUNK, 1), 0)
    s_col = lax.broadcasted_iota(jnp.int32, (1, GLA_CHUNK), 1)
    staged = []
    for heads in chunks:
        per_head = []
        for q, k, v, b in heads:
            q_dec = (q * jnp.exp(b)).astype(BF16)
            att = _dot_nt(q_dec, (k * jnp.exp(-b)).astype(BF16))
            b_last = b[GLA_CHUNK - 1:GLA_CHUNK]
            vb = v.astype(BF16)
            grow = _dot_tn(vb, (k * jnp.exp(b_last - b)).astype(BF16))
            per_head.append((q_dec, att, vb, jnp.exp(b_last), grow))
        staged.append(per_head)
    from_state = []
    for per_head in staged:
        from_state.append([_dot_nt(q_dec, s_t.astype(BF16)) for (q_dec, *_), s_t in zip(per_head, states)])
        states = [s_t * decay + grow for (_, _, _, decay, grow), s_t in zip(per_head, states)]
    outs = []
    for per_head, o_states in zip(staged, from_state):
        outs.append([_dot(jnp.where(s_col <= t_row, att, 0.0).astype(BF16), vb) + o_state
                     for (_, att, vb, _, _), o_state in zip(per_head, o_states)])
    return outs, states


def _gla_body(h_ref, g2_ref, g3_ref, win_ref, wa_ref, wgu_ref, bg_ref, gn_ref, wout_ref, tri_ref,
              o_ref, s_ref, p_ref, b_ref, oc_ref, mild_ref, *, tb):
    jb = pl.program_id(1)

    @pl.when(jb == 0)
    def _():
        s_ref[...] = jnp.zeros_like(s_ref)

    half = tb // 2 if tb % (2 * GLA_CHUNK) == 0 else tb
    tri = tri_ref[...]

    def project(lo):
        ub = _rms(h_ref[0, lo:lo + half, :], g2_ref[...]).astype(BF16)
        yield
        p = _dot(ub, win_ref[...])
        a = _dot(ub, wa_ref[...]).astype(BF16)
        pos = jb * tb + lo + lax.broadcasted_iota(jnp.int32, (half, 1), 0)
        valid = pos >= PAD
        rows = slice(lo, lo + half)
        p_ref[rows, :GLA_DK] = p[:, :GLA_DK] * (GLA_HK ** -0.5)
        p_ref[rows, GLA_DK:2 * GLA_DK] = jnp.where(valid, p[:, GLA_DK:2 * GLA_DK], 0.0)
        p_ref[rows, 2 * GLA_DK:2 * GLA_DK + GLA_DV] = jnp.where(valid, p[:, 2 * GLA_DK:2 * GLA_DK + GLA_DV], 0.0)
        p_ref[rows, 2 * GLA_DK + GLA_DV:] = p[:, 2 * GLA_DK + GLA_DV:]
        yield
        x = _dot(a, wgu_ref[...]) + bg_ref[...]
        la = _log_sigmoid(x) * (1.0 / GLA_GATE_NORMALIZER)
        for c in range(half // GLA_CHUNK):
            b_c = sum(_dot(tri, part) for part in _split3(la[c * GLA_CHUNK:(c + 1) * GLA_CHUNK]))
            b_ref[lo + c * GLA_CHUNK:lo + (c + 1) * GLA_CHUNK, :] = b_c
            mild_ref[lo // GLA_CHUNK + c] = (jnp.min(b_c[GLA_CHUNK - 1:]) > -GLA_MILD_DECAY).astype(jnp.int32)

    _run_interleaved([project(lo) for lo in range(0, tb, half)])

    n_chunks = tb // GLA_CHUNK
    per_iter = next(c for c in (3, 2, 1) if n_chunks % c == 0)

    def chunk_group(ci, carry):
        starts = [pl.multiple_of((ci * per_iter + r) * GLA_CHUNK, GLA_CHUNK) for r in range(per_iter)]
        mild = functools.reduce(jnp.logical_and, [mild_ref[ci * per_iter + r] > 0 for r in range(per_iter)])

        def load(r0):
            rows = pl.ds(r0, GLA_CHUNK)
            b_all = b_ref[rows, :]
            return [(p_ref[rows, hh * GLA_HK:(hh + 1) * GLA_HK],
                     p_ref[rows, GLA_DK + hh * GLA_HK:GLA_DK + (hh + 1) * GLA_HK],
                     p_ref[rows, 2 * GLA_DK + hh * GLA_HV:2 * GLA_DK + (hh + 1) * GLA_HV],
                     b_all[:, hh * GLA_HK:(hh + 1) * GLA_HK]) for hh in range(GLA_HEADS)]

        def store(r0, outs):
            for hh, o in enumerate(outs):
                oc_ref[pl.ds(r0, GLA_CHUNK), hh * GLA_HV:(hh + 1) * GLA_HV] = o

        @pl.when(mild)
        def _():
            outs, states = _gla_chunks_mild([load(r0) for r0 in starts], [s_ref[hh] for hh in range(GLA_HEADS)])
            for r0, chunk_outs in zip(starts, outs):
                store(r0, chunk_outs)
            for hh, s_new in enumerate(states):
                s_ref[hh] = s_new

        @pl.when(jnp.logical_not(mild))
        def _():
            for r0 in starts:
                outs = []
                for hh, head in enumerate(load(r0)):
                    o, s_new = _gla_chunk_head(*head, s_ref[hh])
                    s_ref[hh] = s_new
                    outs.append(o)
                store(r0, outs)

        return carry

    lax.fori_loop(0, n_chunks // per_iter, chunk_group, 0)

    gn = gn_ref[...]

    def finish(lo):
        rows = slice(lo, lo + half)
        normed = []
        for hh in range(GLA_HEADS):
            oh = oc_ref[rows, hh * GLA_HV:(hh + 1) * GLA_HV]
            normed.append(oh * lax.rsqrt(jnp.mean(oh * oh, axis=-1, keepdims=True) + RMS_EPS) * gn)
        gate = p_ref[rows, 2 * GLA_DK + GLA_DV:]
        y = (jnp.concatenate(normed, axis=-1) * (gate * _sigmoid(gate))).astype(BF16)
        yield
        z = _dot(y, wout_ref[...])
        yield
        o_ref[0, rows, :] = h_ref[0, rows, :] + _rms(z, g3_ref[...])

    _run_interleaved([finish(lo) for lo in range(0, tb, half)])


def _gla_mixer(h3, g_pre, g_post, w_in, w_a, w_gu, b_gate, g_norm, w_out, tri, *, tb):
    bsz, lp, _ = h3.shape
    blk = pl.BlockSpec((1, tb, D_MODEL), lambda b, j: (b, j, 0))
    n_in = 2 * GLA_DK + 2 * GLA_DV
    return pl.pallas_call(
        functools.partial(_gla_body, tb=tb),
        grid=(bsz, lp // tb),
        in_specs=[blk, _const_spec((1, D_MODEL)), _const_spec((1, D_MODEL)),
                  _const_spec((D_MODEL, n_in)), _const_spec((D_MODEL, LANES)),
                  _const_spec((LANES, GLA_DK)), _const_spec((1, GLA_DK)), _const_spec((1, GLA_HV)),
                  _const_spec((GLA_DV, D_MODEL)), _const_spec((GLA_CHUNK, GLA_CHUNK))],
        out_specs=blk,
        out_shape=jax.ShapeDtypeStruct(h3.shape, F32),
        scratch_shapes=[pltpu.VMEM((GLA_HEADS, GLA_HV, GLA_HK), F32),
                        pltpu.VMEM((tb, n_in), F32),
                        pltpu.VMEM((tb, GLA_DK), F32),
                        pltpu.VMEM((tb, GLA_DV), F32),
                        pltpu.SMEM((tb // GLA_CHUNK,), jnp.int32)],
        compiler_params=_params(2),
        name="gla_mixer",
    )(h3, g_pre, g_post, w_in, w_a, w_gu, b_gate, g_norm, w_out, tri)


def _proj_in_body(h_ref, g_ref, w_ref, o_ref):
    tm = h_ref.shape[0]
    n_split = next(c for c in (3, 2, 1) if tm % (SUBLANES_BF16 * c) == 0)
    rows = tm // n_split

    def chain(lo):
        xb = _rms(h_ref[lo:lo + rows, :], g_ref[...]).astype(BF16)
        yield
        o_ref[lo:lo + rows, :] = _dot(xb, w_ref[...]).astype(o_ref.dtype)

    _run_interleaved([chain(c * rows) for c in range(n_split)])


def _proj_in(h, g, w, *, tm):
    t, n = h.shape[0], w.shape[1]
    return pl.pallas_call(
        _proj_in_body,
        grid=(t // tm,),
        in_specs=[pl.BlockSpec((tm, D_MODEL), lambda i: (i, 0)), _const_spec((1, D_MODEL)),
                  _const_spec(w.shape)],
        out_specs=pl.BlockSpec((tm, n), lambda i: (i, 0)),
        out_shape=jax.ShapeDtypeStruct((t, n), BF16),
        compiler_params=_params(1),
        name="proj_in",
    )(h, g, w)


def _proj_out_body(h_ref, x_ref, g_ref, w_ref, o_ref):
    o_ref[...] = h_ref[...] + _rms(_dot(x_ref[...], w_ref[...]), g_ref[...])


def _proj_out(h, x, g, w, *, tm):
    t, kdim = x.shape
    row = pl.BlockSpec((tm, D_MODEL), lambda i: (i, 0))
    return pl.pallas_call(
        _proj_out_body,
        grid=(t // tm,),
        in_specs=[row, pl.BlockSpec((tm, kdim), lambda i: (i, 0)), _const_spec((1, D_MODEL)),
                  _const_spec(w.shape)],
        out_specs=row,
        out_shape=jax.ShapeDtypeStruct(h.shape, F32),
        compiler_params=_params(1),
        name="proj_out",
    )(h, x, g, w)


SB_WIDE = 3
SB_DEAD = -104.0
SB_GROUP = 3


def _sb_scores(qs, kwin, mask):
    z = _dot_nt(qs, kwin)
    ls = _log_sigmoid(z)
    return ls, jnp.where(mask, ls - z, 0.0)


def _sb_suffix(l1m, u):
    hi = l1m.astype(BF16)
    lo = (l1m - hi.astype(F32)).astype(BF16)
    return _dot(jnp.concatenate([hi, lo], axis=1), u)


def _sb_body(q_ref, k_ref, v_ref, u_ref, o_ref, *, nq):
    lane = lax.broadcasted_iota(jnp.int32, (1, LANES), 1)
    low = lane < SB_HEAD_DIM
    zero = jnp.zeros((), BF16)
    row = lax.broadcasted_iota(jnp.int32, (2 * SB_BLOCK, 1), 0) % SB_BLOCK
    col = lax.broadcasted_iota(jnp.int32, (1, SB_BLOCK), 1)
    u = u_ref[...]

    def stacked_q(m):
        q = q_ref[0, pl.ds(pl.multiple_of(m * SB_BLOCK, SB_BLOCK), SB_BLOCK), :] * (SB_HEAD_DIM ** -0.5)
        return jnp.concatenate([jnp.where(low, q, zero), jnp.where(low, zero, q)], axis=0)

    def wide(m, near_start):
        qs = stacked_q(m)
        t_glob = m * SB_BLOCK + row
        kparts, vparts, masks = [], [], []
        for bi in range(SB_WIDE):
            jv = m - (SB_WIDE - 1) + bi
            k0 = pl.multiple_of(jnp.maximum(jv, 0) * SB_BLOCK, SB_BLOCK)
            kparts.append(k_ref[0, pl.ds(k0, SB_BLOCK), :])
            vparts.append(v_ref[0, pl.ds(k0, SB_BLOCK), :])
            s_glob = jv * SB_BLOCK + col
            causal = s_glob < t_glob if bi == SB_WIDE - 1 else None
            if near_start:
                real = jnp.broadcast_to(s_glob >= PAD, (2 * SB_BLOCK, SB_BLOCK))
                causal = real if causal is None else causal & real
            masks.append(causal)
        z = _dot_nt(qs, jnp.concatenate(kparts, axis=0))
        yield
        lsp, cs = [], []
        for bi in range(SB_WIDE):
            zb = z[:, bi * SB_BLOCK:(bi + 1) * SB_BLOCK]
            ls = _log_sigmoid(zb)
            l1m = ls - zb
            if masks[bi] is not None:
                l1m = jnp.where(masks[bi], l1m, 0.0)
            lsp.append(ls)
            cs.append(_sb_suffix(l1m, u))
        yield
        ws, later = [], None
        for bi in reversed(range(SB_WIDE)):
            tail = cs[bi][:, :SB_BLOCK]
            w = jnp.exp(lsp[bi] + (tail if later is None else tail + later))
            ws.append(w if masks[bi] is None else jnp.where(masks[bi], w, 0.0))
            tot = cs[bi][:, SB_BLOCK:]
            later = tot if later is None else later + tot
        w = jnp.concatenate(ws[::-1], axis=1)
        acc = _dot(w.astype(BF16), jnp.concatenate(vparts, axis=0))
        return later, acc

    def alive(c):
        return (jnp.max(c) >= SB_DEAD).astype(jnp.int32)

    def finish(m, car, acc):
        def cond(st):
            return (st[0] >= 0) & (st[1] > 0)

        def body(st):
            j, _, car, acc = st
            k0 = pl.multiple_of(j * SB_BLOCK, SB_BLOCK)
            mask = jnp.broadcast_to(k0 + col >= PAD, (2 * SB_BLOCK, SB_BLOCK))
            ls, l1m = _sb_scores(stacked_q(m), k_ref[0, pl.ds(k0, SB_BLOCK), :], mask)
            cs = _sb_suffix(l1m, u)
            w = jnp.where(mask, jnp.exp(ls + cs[:, :SB_BLOCK] + car), 0.0)
            acc = acc + _dot(w.astype(BF16), v_ref[0, pl.ds(k0, SB_BLOCK), :])
            car = car + cs[:, SB_BLOCK:]
            return j - 1, alive(car), car, acc

        _, _, _, acc = lax.while_loop(cond, body, (m - SB_WIDE, alive(car), car, acc))
        put(m, acc)

    def put(m, acc):
        out = jnp.where(low, acc[:SB_BLOCK], acc[SB_BLOCK:])
        o_ref[0, pl.ds(pl.multiple_of(m * SB_BLOCK, SB_BLOCK), SB_BLOCK), :] = out.astype(o_ref.dtype)

    def group(ms, near_start):
        if not ms:
            return
        states = _run_interleaved([wide(m, near_start) for m in ms])
        for m, (_, acc) in zip(ms, states):
            put(m, acc)
        worst = functools.reduce(jnp.maximum, [car for car, _ in states])
        more = (jnp.max(worst) >= SB_DEAD) & (ms[-1] >= SB_WIDE)

        @pl.when(more)
        def _():
            for m, (car, acc) in zip(ms, states):
                finish(m, car, acc)

    n_start = -(-SB_WIDE // SB_GROUP)
    for gi in range(n_start):
        group([jnp.int32(m) for m in range(gi * SB_GROUP, (gi + 1) * SB_GROUP) if m < nq], True)

    def steady(gi, carry):
        group([gi * SB_GROUP + r for r in range(SB_GROUP)], False)
        return carry

    lax.fori_loop(n_start, nq // SB_GROUP, steady, 0)
    group([jnp.int32(m) for m in range(max(nq // SB_GROUP, n_start) * SB_GROUP, nq)], False)


def _sb_attention(qkv3, u2):
    bsz, lp, _ = qkv3.shape
    n_pairs = D_MODEL // LANES
    assert lp >= SB_WIDE * SB_BLOCK

    def col_spec(offset):
        return pl.BlockSpec((1, lp, LANES), lambda b, p: (b, 0, offset + p))

    return pl.pallas_call(
        functools.partial(_sb_body, nq=lp // SB_BLOCK),
        grid=(bsz, n_pairs),
        in_specs=[col_spec(0), col_spec(n_pairs), col_spec(2 * n_pairs),
                  _const_spec((2 * SB_BLOCK, 2 * SB_BLOCK))],
        out_specs=pl.BlockSpec((1, lp, LANES), lambda b, p: (b, 0, p)),
        out_shape=jax.ShapeDtypeStruct((bsz, lp, D_MODEL), BF16),
        compiler_params=_params(2),
        name="sb_attention",
    )(qkv3, qkv3, qkv3, u2)


def _suffix_sum_matrix():
    j = jnp.arange(SB_BLOCK)[:, None]
    s = jnp.arange(2 * SB_BLOCK)[None, :]
    u = ((j > s) | (s >= SB_BLOCK)).astype(BF16)
    return jnp.concatenate([u, u], axis=0)


def kernel(x, meta_tokens, norm_gains, ffn_w_in, ffn_w_out, conv_w_in, conv_w, conv_w_out,
           gla_w_in, gla_w_gate_up, gla_b_gate, gla_norm, gla_w_out, sb_w_in, sb_w_out):
    bsz, seq, _ = x.shape
    depth = norm_gains.shape[0]
    lp = PAD + N_META + seq
    assert lp % SB_BLOCK == 0 and lp % GLA_CHUNK == 0
    t = bsz * lp
    tm = _row_tile(t, (512, 384, 256, 128))
    tm_enter = _row_tile(t, (768, 512, 384, 256, 128))
    tm_ffn = _row_tile(t, (1056, 768, 512, 384, 256, 128))
    tb = _row_tile(lp, (384, 256, 128, 64))

    seq_head = jnp.concatenate([jnp.zeros((PAD, D_MODEL), x.dtype), meta_tokens.astype(x.dtype)], axis=0)
    pps = lp // SB_BLOCK
    tm_out = _row_tile(bsz * seq, (512, 256, 128))

    gains = norm_gains.reshape(depth, -1, 1, D_MODEL)
    ffn_w_in, ffn_w_out = ffn_w_in.astype(BF16), ffn_w_out.astype(BF16)
    tri = (jnp.arange(GLA_CHUNK)[:, None] >= jnp.arange(GLA_CHUNK)[None, :]).astype(BF16)
    u2 = _suffix_sum_matrix()

    h = x.reshape(bsz * seq, D_MODEL)
    for i in range(depth):
        g = gains[i]
        if i == 0:
            h = _ffn(h, g[0], g[1], ffn_w_in, ffn_w_out, i, 0, tm=tm_enter, mode="enter", seq_head=seq_head,
                     pieces_per_seq=pps)
        else:
            h = _ffn(h, g[0], g[1], ffn_w_in, ffn_w_out, i, 0, tm=tm_ffn)
        kind, j = i % 3, i // 3
        if kind == 0:
            cw = jnp.pad(conv_w[j], ((0, SUBLANES - CONV_WIDTH), (0, 0)))
            h = _conv_mixer(h, g[2], g[3], cw, conv_w_in[j].astype(BF16), conv_w_out[j].astype(BF16),
                            tm=tm_ffn, lp=lp)
        elif kind == 1:
            n_main = 2 * GLA_DK + 2 * GLA_DV
            w_a = jnp.pad(gla_w_in[j][:, n_main:], ((0, 0), (0, LANES - GLA_RANK))).astype(BF16)
            w_gu = jnp.pad(gla_w_gate_up[j], ((0, LANES - GLA_RANK), (0, 0))).astype(BF16)
            h = _gla_mixer(h.reshape(bsz, lp, D_MODEL), g[2], g[3], gla_w_in[j][:, :n_main].astype(BF16),
                           w_a, w_gu, gla_b_gate[j][None], gla_norm[j][None], gla_w_out[j].astype(BF16),
                           tri, tb=tb).reshape(t, D_MODEL)
        else:
            qkv = _proj_in(h, g[2], sb_w_in[j].astype(BF16), tm=tm_ffn)
            o = _sb_attention(qkv.reshape(bsz, lp, 3 * D_MODEL), u2)
            mix = (o.reshape(t, D_MODEL), g[3], sb_w_out[j].astype(BF16))
            if i == depth - 1:
                h = _proj_out(h, *mix, tm=tm)
        if i == depth - 1:
            h = _ffn(h, g[4], g[5], ffn_w_in, ffn_w_out, i, 1, tm=tm_out, mode="leave", pieces_per_seq=pps)
        else:
            h = _ffn(h, g[4], g[5], ffn_w_in, ffn_w_out, i, 1, tm=tm_ffn, mix=mix if kind == 2 else None)
    return h.reshape(bsz, seq, D_MODEL)
```
